```python
import jax
import jax.numpy as jnp
from jax import lax
import numpy as np

D_MODEL = 4096
BATCH = 4
SEQ = 2048
DEPTH = 4
DEC_BATCH = 32
DEC_SEQ = 4
PAST_LEN = 8192
PAGE_SIZE = 128

GLA_WIDTH = D_MODEL // 2
GLA_HEADS = 4
GLA_DK = GLA_WIDTH // (2 * GLA_HEADS)
GLA_DV = GLA_WIDTH // GLA_HEADS
GATE_RANK = 16
GATE_TEMP = 16.0
GLA_CHUNK = 16
SWA_WIDTH = D_MODEL - GLA_WIDTH
SWA_HEAD_DIM = 64
SWA_HEADS = SWA_WIDTH // SWA_HEAD_DIM
SWA_KV_HEADS = SWA_HEADS // 8
SWA_GROUP = SWA_HEADS // SWA_KV_HEADS
WINDOW = 128
N_EXPERTS = 16
N_GROUPS = 4
EXPERTS_PER_GROUP = N_EXPERTS // N_GROUPS
TOP_K = 2
D_FF_EXPERT = D_MODEL // 4
MOE_BLOCK = 64
EPS = 1e-6
NEG_INF = -1e30
IN_SIZES = (GLA_HEADS * GLA_DK, GLA_HEADS * GLA_DK, GLA_HEADS * GLA_DV, GLA_WIDTH, GATE_RANK,
            SWA_HEADS * SWA_HEAD_DIM, SWA_KV_HEADS * SWA_HEAD_DIM, SWA_KV_HEADS * SWA_HEAD_DIM)
IN_COLS = sum(IN_SIZES)

kernel_name = 'hybrid_gla_swa_sink_moe_adaln_step'


def rms_norm(x, g):
    x32 = x.astype(jnp.float32)
    y = x32 * lax.rsqrt(jnp.mean(x32 * x32, axis=-1, keepdims=True) + EPS)
    return (y * g.astype(jnp.float32)).astype(x.dtype)


def gla_recurrence(q, k, v, log_a, s0):
    B, T, H, _ = q.shape
    dv = v.shape[-1]
    L = min(GLA_CHUNK, T)
    n = -(-T // L)
    pad = n * L - T

    def to_chunks(a):
        a = jnp.pad(a.astype(jnp.float32), ((0, 0), (0, pad), (0, 0), (0, 0)))
        return a.reshape(B, n, L, H, a.shape[-1]).transpose(1, 0, 3, 2, 4)

    xs = tuple(to_chunks(a) for a in (q, k, v, log_a))
    causal = jnp.tril(jnp.ones((L, L), dtype=bool))

    def step(S, inp):
        qc, kc, vc, ac = inp
        b = jnp.cumsum(ac, axis=2)
        q_dec = qc * jnp.exp(b)
        k_dec = kc * jnp.exp(-b)
        att = jnp.where(causal, jnp.einsum('bhid,bhjd->bhij', q_dec, k_dec), 0.0)
        o = jnp.einsum('bhij,bhjv->bhiv', att, vc) + jnp.einsum('bhid,bhdv->bhiv', q_dec, S)
        b_end = b[:, :, -1:, :]
        S = (jnp.exp(b_end)[:, :, 0, :, None] * S
             + jnp.einsum('bhjd,bhjv->bhdv', kc * jnp.exp(b_end - b), vc))
        return S, o

    S, o = lax.scan(step, s0.astype(jnp.float32), xs)
    o = o.transpose(1, 0, 3, 2, 4).reshape(B, n * L, H, dv)[:, :T]
    return o, S


def sink_attention(q, k, v, q_pos, k_pos, sinks):
    s = jnp.einsum('bnqkgd,bnskd->bnkgqs', q, k).astype(jnp.float32) * (SWA_HEAD_DIM ** -0.5)
    rel = q_pos[:, :, None] - k_pos[:, None, :]
    mask = (rel >= 0) & (rel <= WINDOW) & (k_pos[:, None, :] >= 0)
    s = jnp.where(mask[None, :, None, None], s, NEG_INF)
    sink = jnp.broadcast_to(
        sinks.astype(jnp.float32).reshape(SWA_KV_HEADS, SWA_GROUP)[None, None, :, :, None, None],
        s.shape[:-1] + (1,))
    p = jax.nn.softmax(jnp.concatenate([s, sink], axis=-1), axis=-1)[..., :-1]
    return jnp.einsum('bnkgqs,bnskd->bnqkgd', p.astype(v.dtype), v)


def swa_prompt(q, k, v, sinks):
    B, T = q.shape[:2]
    nb = T // WINDOW
    qb = q.reshape(B, nb, WINDOW, SWA_KV_HEADS, SWA_GROUP, SWA_HEAD_DIM)

    def band(a):
        ab = a.reshape(B, nb, WINDOW, SWA_KV_HEADS, SWA_HEAD_DIM)
        prev = jnp.concatenate([jnp.zeros_like(ab[:, :1]), ab[:, :-1]], axis=1)
        return jnp.concatenate([prev, ab], axis=2)

    start = jnp.arange(nb)[:, None] * WINDOW
    q_pos = start + jnp.arange(WINDOW)[None, :]
    k_pos = start - WINDOW + jnp.arange(2 * WINDOW)[None, :]
    o = sink_attention(qb, band(k), band(v), q_pos, k_pos, sinks)
    return o.reshape(B, T, SWA_HEADS * SWA_HEAD_DIM), k[:, -WINDOW:], v[:, -WINDOW:]


def swa_sample(q, k, v, k_buf, v_buf, sinks):
    B, T = q.shape[:2]
    n_buf = k_buf.shape[1]
    kk = jnp.concatenate([k_buf.astype(k.dtype), k], axis=1)
    vv = jnp.concatenate([v_buf.astype(v.dtype), v], axis=1)
    q_pos = (n_buf + jnp.arange(T))[None, :]
    k_pos = jnp.arange(n_buf + T)[None, :]
    o = sink_attention(q[:, None], kk[:, None], vv[:, None], q_pos, k_pos, sinks)
    return o.reshape(B, T, SWA_HEADS * SWA_HEAD_DIM), kk[:, -n_buf:], vv[:, -n_buf:]


def group_limited_route(h, w_router, b_router):
    logits = h.astype(jnp.float32) @ w_router.astype(jnp.float32) + b_router.astype(jnp.float32)
    probs = jax.nn.softmax(logits, axis=-1)
    pg = probs.reshape(-1, N_GROUPS, EXPERTS_PER_GROUP)
    group_score = jnp.sum(lax.top_k(pg, TOP_K)[0], axis=-1)
    g_sel = jnp.argmax(group_score, axis=-1)
    in_group = jnp.take_along_axis(pg, g_sel[:, None, None], axis=1)[:, 0]
    vals, idx = lax.top_k(in_group, TOP_K)
    expert_idx = (g_sel[:, None] * EXPERTS_PER_GROUP + idx).astype(jnp.int32)
    gates = vals / jnp.sum(vals, axis=-1, keepdims=True)
    return expert_idx, gates


def moe_ffn(h, expert_idx, gates, w_g, w_u, w_d):
    N, D = h.shape
    E = w_g.shape[0]
    M = N * TOP_K
    flat_e = expert_idx.reshape(M)
    flat_tok = jnp.repeat(jnp.arange(N, dtype=jnp.int32), TOP_K)
    flat_gate = gates.reshape(M)
    order = jnp.argsort(flat_e)
    se = flat_e[order]
    counts = jnp.bincount(flat_e, length=E)
    padded = (counts + MOE_BLOCK - 1) // MOE_BLOCK * MOE_BLOCK
    pad_end = jnp.cumsum(padded)
    pad_start = pad_end - padded
    start = jnp.cumsum(counts) - counts
    dest = pad_start[se] + jnp.arange(M) - start[se]
    nb = (M + E * (MOE_BLOCK - 1) + MOE_BLOCK - 1) // MOE_BLOCK
    P = nb * MOE_BLOCK
    slot_tok = jnp.full((P,), N, jnp.int32).at[dest].set(flat_tok[order])
    block_e = jnp.minimum(jnp.searchsorted(pad_end, jnp.arange(nb) * MOE_BLOCK, side='right'), E - 1)
    xb = jnp.concatenate([h, jnp.zeros((1, D), h.dtype)], axis=0)[slot_tok].reshape(nb, MOE_BLOCK, D)

    def expert_block(args):
        xblk, e = args
        return (jax.nn.silu(xblk @ w_g[e]) * (xblk @ w_u[e])) @ w_d[e]

    yb = lax.map(expert_block, (xb, block_e)).reshape(P, D)
    y_sorted = yb[dest] * flat_gate[order][:, None].astype(yb.dtype)
    return jnp.zeros((N, D), yb.dtype).at[flat_tok[order]].add(y_sorted)


def trunk_layer(x, c, gla_s0, k_buf, v_buf, w_ada, b_ada, g_norm1, g_norm2, w_in, w_gate_up, b_gate,
                g_gla_out, g_q_norm, g_k_norm, sinks, w_out, w_router, b_router, w_e_gate, w_e_up, w_e_down):
    B, T, D = x.shape
    mod = jax.nn.silu(c) @ w_ada + b_ada
    sh1, sc1, gt1, sh2, sc2, gt2 = jnp.split(mod[:, None, :], 6, axis=-1)
    h = rms_norm(x, g_norm1) * (1 + sc1) + sh1
    proj = h @ w_in
    pts = np.cumsum(IN_SIZES)[:-1].tolist()
    gq, gk, gv, gg, glr, sq, sk, sv = jnp.split(proj, pts, axis=-1)

    gq = gq.reshape(B, T, GLA_HEADS, GLA_DK) * (GLA_DK ** -0.5)
    gk = gk.reshape(B, T, GLA_HEADS, GLA_DK)
    gv = gv.reshape(B, T, GLA_HEADS, GLA_DV)
    log_a = (jax.nn.log_sigmoid((glr @ w_gate_up + b_gate).astype(jnp.float32)) / GATE_TEMP
             ).reshape(B, T, GLA_HEADS, GLA_DK)
    if gla_s0 is None:
        gla_s0 = jnp.zeros((B, GLA_HEADS, GLA_DK, GLA_DV), jnp.float32)
    o_gla, s_new = gla_recurrence(gq, gk, gv, log_a, gla_s0)
    o_gla = rms_norm(o_gla, g_gla_out).astype(x.dtype).reshape(B, T, GLA_WIDTH) * jax.nn.silu(gg)

    q = rms_norm(sq.reshape(B, T, SWA_HEADS, SWA_HEAD_DIM), g_q_norm)
    q = q.reshape(B, T, SWA_KV_HEADS, SWA_GROUP, SWA_HEAD_DIM)
    k = rms_norm(sk.reshape(B, T, SWA_KV_HEADS, SWA_HEAD_DIM), g_k_norm)
    v = sv.reshape(B, T, SWA_KV_HEADS, SWA_HEAD_DIM)
    if k_buf is None:
        o_swa, k_new, v_new = swa_prompt(q, k, v, sinks)
    else:
        o_swa, k_new, v_new = swa_sample(q, k, v, k_buf, v_buf, sinks)

    mix = jnp.concatenate([o_gla, o_swa.astype(x.dtype)], axis=-1) @ w_out
    x = x + gt1 * mix

    h2 = (rms_norm(x, g_norm2) * (1 + sc2) + sh2).reshape(B * T, D)
    expert_idx, gates = group_limited_route(h2, w_router, b_router)
    y = moe_ffn(h2, expert_idx, gates, w_e_gate, w_e_up, w_e_down).reshape(B, T, D)
    x = x + gt2 * y.astype(x.dtype)
    return x, s_new, k_new, v_new


def setup_inputs(seed: int = 0) -> dict:
    key = jax.random.key(seed)
    ks = jax.random.split(key, 24)
    f32 = jnp.float32

    def nrm(k, shape, scale):
        return jax.random.normal(k, shape, f32) * scale

    swa_buf = min(WINDOW, PAST_LEN)
    return {
        'x_prompt': nrm(ks[0], (BATCH, SEQ, D_MODEL), 1.0),
        'x_sample': nrm(ks[1], (DEC_BATCH, DEC_SEQ, D_MODEL), 1.0),
        'c_prompt': nrm(ks[2], (BATCH, D_MODEL), 1.0),
        'c_sample': nrm(ks[3], (DEC_BATCH, D_MODEL), 1.0),
        'state_gla': nrm(ks[4], (DEPTH, DEC_BATCH, GLA_HEADS, GLA_DK, GLA_DV), 0.5),
        'cache_swa_k': nrm(ks[5], (DEPTH, DEC_BATCH, swa_buf, SWA_KV_HEADS, SWA_HEAD_DIM), 1.0),
        'cache_swa_v': nrm(ks[6], (DEPTH, DEC_BATCH, swa_buf, SWA_KV_HEADS, SWA_HEAD_DIM), 1.0),
        'w_ada': nrm(ks[7], (DEPTH, D_MODEL, 6 * D_MODEL), 0.5 * D_MODEL ** -0.5),
        'b_ada': nrm(ks[8], (DEPTH, 6 * D_MODEL), 0.02),
        'g_norm1': 1.0 + nrm(ks[9], (DEPTH, D_MODEL), 0.02),
        'g_norm2': 1.0 + nrm(ks[10], (DEPTH, D_MODEL), 0.02),
        'w_in': nrm(ks[11], (DEPTH, D_MODEL, IN_COLS), D_MODEL ** -0.5),
        'w_gate_up': nrm(ks[12], (DEPTH, GATE_RANK, GLA_HEADS * GLA_DK), GATE_RANK ** -0.5),
        'b_gate': nrm(ks[13], (DEPTH, GLA_HEADS * GLA_DK), 0.1),
        'g_gla_out': 1.0 + nrm(ks[14], (DEPTH, GLA_DV), 0.02),
        'g_q_norm': 1.0 + nrm(ks[15], (DEPTH, SWA_HEAD_DIM), 0.02),
        'g_k_norm': 1.0 + nrm(ks[16], (DEPTH, SWA_HEAD_DIM), 0.02),
        'sinks': nrm(ks[17], (DEPTH, SWA_HEADS), 0.5),
        'w_out': nrm(ks[18], (DEPTH, D_MODEL, D_MODEL), D_MODEL ** -0.5),
        'w_router': nrm(ks[19], (D_MODEL, N_EXPERTS), D_MODEL ** -0.5),
        'b_router': nrm(ks[20], (N_EXPERTS,), 0.01),
        'w_e_gate': nrm(ks[21], (DEPTH, N_EXPERTS, D_MODEL, D_FF_EXPERT), D_MODEL ** -0.5),
        'w_e_up': nrm(ks[22], (DEPTH, N_EXPERTS, D_MODEL, D_FF_EXPERT), D_MODEL ** -0.5),
        'w_e_down': nrm(ks[23], (DEPTH, N_EXPERTS, D_FF_EXPERT, D_MODEL), D_FF_EXPERT ** -0.5),
    }


def reference(x_prompt, x_sample, c_prompt, c_sample, state_gla, cache_swa_k, cache_swa_v,
              w_ada, b_ada, g_norm1, g_norm2, w_in, w_gate_up, b_gate, g_gla_out, g_q_norm, g_k_norm,
              sinks, w_out, w_router, b_router, w_e_gate, w_e_up, w_e_down):
    def layer_weights(l):
        return (w_ada[l], b_ada[l], g_norm1[l], g_norm2[l], w_in[l], w_gate_up[l], b_gate[l],
                g_gla_out[l], g_q_norm[l], g_k_norm[l], sinks[l], w_out[l], w_router, b_router,
                w_e_gate[l], w_e_up[l], w_e_down[l])

    yp, ys = x_prompt, x_sample
    sp_l, kp_l, vp_l, ss_l, ks_l, vs_l = [], [], [], [], [], []
    for l in range(DEPTH):
        lw = layer_weights(l)
        yp, sp, kp, vp = trunk_layer(yp, c_prompt, None, None, None, *lw)
        ys, ss, kss, vss = trunk_layer(ys, c_sample, state_gla[l], cache_swa_k[l], cache_swa_v[l], *lw)
        sp_l.append(sp.astype(state_gla.dtype))
        kp_l.append(kp.astype(cache_swa_k.dtype))
        vp_l.append(vp.astype(cache_swa_v.dtype))
        ss_l.append(ss.astype(state_gla.dtype))
        ks_l.append(kss.astype(cache_swa_k.dtype))
        vs_l.append(vss.astype(cache_swa_v.dtype))
    state_gla_prompt = jnp.stack(sp_l)
    cache_swa_k_prompt = jnp.stack(kp_l)
    cache_swa_v_prompt = jnp.stack(vp_l)
    state_gla_sample = jnp.stack(ss_l)
    cache_swa_k_sample = jnp.stack(ks_l)
    cache_swa_v_sample = jnp.stack(vs_l)
    return (yp, ys, state_gla_prompt, cache_swa_k_prompt, cache_swa_v_prompt,
            state_gla_sample, cache_swa_k_sample, cache_swa_v_sample)
```

```python
import functools

import jax
import jax.numpy as jnp
from jax import lax
from jax.experimental import pallas as pl
from jax.experimental.pallas import tpu as pltpu

F32 = jnp.float32
BF16 = jnp.bfloat16

GLA_HEADS = 4
GATE_RANK = 16
GATE_TEMP = 16.0
SWA_HEAD_DIM = 64
SWA_GROUP = 8
WINDOW = 128
N_GROUPS = 4
TOP_K = 2
EPS = 1e-6
NEG_INF = -1e30

LANES = 128
BF16_SUBLANES = 16
VMEM_LIMIT = 48 * 1024 * 1024

SPAD = BF16_SUBLANES
ROW_BLK = 128
GLA_CHUNK = 128
MM_ROWS = 1088
MOE_BLK = 256
MOE_FC = 512
MOE_DC = 2048


def _cparams(**kw):
    return pltpu.CompilerParams(vmem_limit_bytes=VMEM_LIMIT, **kw)


def _ada_kernel(c_ref, w_ref, b_ref, o_ref):
    c = c_ref[...]
    a = (c * (1.0 / (1.0 + jnp.exp(-c)))).astype(BF16)
    o_ref[...] = jnp.dot(a, w_ref[...].astype(BF16), preferred_element_type=F32) + b_ref[...]


def _ada_mod(c_all, w_ada, b_ada):
    L, D, N = w_ada.shape
    R = c_all.shape[0]
    tn = 512
    return pl.pallas_call(
        _ada_kernel,
        grid=(L, N // tn),
        in_specs=[pl.BlockSpec((R, D), lambda l, j: (0, 0)),
                  pl.BlockSpec((None, D, tn), lambda l, j: (l, 0, j)),
                  pl.BlockSpec((None, 1, tn), lambda l, j: (l, 0, j))],
        out_specs=pl.BlockSpec((None, R, tn), lambda l, j: (l, 0, j)),
        out_shape=jax.ShapeDtypeStruct((L, R, N), F32),
        compiler_params=_cparams(),
        name="ada_mod",
    )(c_all, w_ada, b_ada.reshape(L, 1, N))


class _Rows:
    def __init__(self, B, T, SB, ST):
        self.B, self.T, self.SB, self.ST = B, T, SB, ST
        self.n_prompt = B * T
        self.n_sample = SB * SPAD
        self.M = self.n_prompt + self.n_sample
        assert T % ROW_BLK == 0 and self.n_sample % ROW_BLK == 0 and ST <= SPAD
        self.npb = self.n_prompt // ROW_BLK
        self.nsb = self.n_sample // ROW_BLK
        self.nblk = self.npb + self.nsb
        self.blk_per_seq = T // ROW_BLK


def _mod_specs(rows, D, col):
    npb, bps, B = rows.npb, rows.blk_per_seq, rows.B
    p_spec = pl.BlockSpec((None, 1, D), lambda i: (jnp.minimum(i // bps, B - 1), 0, col))
    s_spec = pl.BlockSpec((ROW_BLK, D), lambda i: (jnp.maximum(i - npb, 0), col))
    return p_spec, s_spec


def _pick_mod(i, npb, p_ref, s_ref):
    return jnp.where(i < npb, p_ref[...], s_ref[...])


def _rms(x, g):
    return x * lax.rsqrt(jnp.mean(x * x, axis=-1, keepdims=True) + EPS) * g


def _norm_mod_kernel(x_ref, g_ref, shp_ref, shs_ref, scp_ref, scs_ref, h_ref, *, npb):
    i = pl.program_id(0)
    y = _rms(x_ref[...], g_ref[...])
    sh = _pick_mod(i, npb, shp_ref, shs_ref)
    sc = _pick_mod(i, npb, scp_ref, scs_ref)
    h_ref[...] = (y * (1.0 + sc) + sh).astype(BF16)


def _norm_mod(rows, x, g, l, modp, mods):
    M, D = x.shape
    shp, shs = _mod_specs(rows, D, 0)
    scp, scs = _mod_specs(rows, D, 1)
    row_spec = pl.BlockSpec((ROW_BLK, D), lambda i: (i, 0))
    return pl.pallas_call(
        functools.partial(_norm_mod_kernel, npb=rows.npb),
        grid=(rows.nblk,),
        in_specs=[row_spec, pl.BlockSpec((None, 1, D), lambda i: (l, 0, 0)), shp, shs, scp, scs],
        out_specs=row_spec,
        out_shape=jax.ShapeDtypeStruct((M, D), BF16),
        compiler_params=_cparams(),
        name="norm_mod",
    )(x, g, modp, mods, modp, mods)


def _mm_kernel(*refs, n_lhs, with_tail):
    x_refs, w_ref, o_ref = refs[:n_lhs], refs[n_lhs], refs[n_lhs + 1]
    wbf_ref = refs[-1]

    @pl.when(pl.program_id(1) == 0)
    def _():
        wbf_ref[...] = w_ref[...].astype(BF16)

    acc = None
    k0 = 0
    for x_ref in x_refs:
        kk = x_ref.shape[1]
        part = jnp.dot(x_ref[...], wbf_ref[k0:k0 + kk, :], preferred_element_type=F32)
        acc = part if acc is None else acc + part
        k0 += kk
    o_ref[...] = acc.astype(o_ref.dtype)
    if with_tail:
        @pl.when(pl.program_id(1) == pl.num_programs(1) - 1)
        def _():
            refs[n_lhs + 2][...] = acc


def _matmul(xs, w, w_lead, col_blk0, n_cols, tn, tm, out_dtype, name, with_tail=False):
    M = xs[0].shape[0]
    K = sum(x.shape[1] for x in xs)
    if w.ndim == 3:
        w_spec = pl.BlockSpec((None, K, tn), lambda j, i: (w_lead, 0, col_blk0 + j))
    else:
        w_spec = pl.BlockSpec((K, tn), lambda j, i: (0, col_blk0 + j))
    x_specs = [pl.BlockSpec((tm, x.shape[1]), lambda j, i: (i, 0)) for x in xs]
    out_specs = [pl.BlockSpec((tm, tn), lambda j, i: (i, j))]
    out_shape = [jax.ShapeDtypeStruct((M, n_cols), out_dtype)]
    if with_tail:
        out_specs.append(pl.BlockSpec((tm, tn), lambda j, i: (0, j)))
        out_shape.append(jax.ShapeDtypeStruct((tm, n_cols), F32))
    return pl.pallas_call(
        functools.partial(_mm_kernel, n_lhs=len(xs), with_tail=with_tail),
        grid=(n_cols // tn, M // tm),
        in_specs=x_specs + [w_spec],
        out_specs=out_specs,
        out_shape=out_shape,
        scratch_shapes=[pltpu.VMEM((K, tn), BF16)],
        compiler_params=_cparams(),
        name=name,
    )(*xs, w)


def _gla_kernel(*refs, C, valid, has_s0, dk, dv):
    if has_s0:
        (q_ref, k_ref, v_ref, gg_ref, glr_ref, wgu_ref, bg_ref, go_ref, s0_ref, _o_prev,
         o_ref, sf_ref, S) = refs
    else:
        (q_ref, k_ref, v_ref, gg_ref, glr_ref, wgu_ref, bg_ref, go_ref, o_ref, sf_ref, S) = refs
    c = pl.program_id(2)

    @pl.when(c == 0)
    def _():
        if has_s0:
            S[...] = s0_ref[...]
        else:
            S[...] = jnp.zeros_like(S)

    z = jnp.dot(glr_ref[...], wgu_ref[...], preferred_element_type=F32) + bg_ref[...]
    la = (jnp.minimum(z, 0.0) - jnp.log(1.0 + jnp.exp(-jnp.abs(z)))) * (1.0 / GATE_TEMP)
    q = q_ref[...].astype(F32) * (dk ** -0.5)
    k = k_ref[...].astype(F32)
    if valid < C:
        live = lax.broadcasted_iota(jnp.int32, (C, 1), 0) < valid
        la = jnp.where(live, la, 0.0)
        k = jnp.where(live, k, 0.0)

    la_hi = la.astype(BF16).astype(F32)
    la_lo = (la - la_hi).astype(BF16).astype(F32)
    ri = lax.broadcasted_iota(jnp.int32, (C, C), 0)
    ci = lax.broadcasted_iota(jnp.int32, (C, C), 1)
    tril = ri >= ci
    trilf = tril.astype(F32)
    b = (jnp.dot(trilf, la_hi, preferred_element_type=F32)
         + jnp.dot(trilf, la_lo, preferred_element_type=F32))

    q_dec = (q * jnp.exp(b)).astype(BF16)
    if C > SPAD:
        r = b[C // 2:C // 2 + 1, :]
        q_in = (q * jnp.exp(b - r)).astype(BF16)
        k_in = (k * jnp.exp(r - b)).astype(BF16)
    else:
        q_in = q_dec
        k_in = (k * jnp.exp(-b)).astype(BF16)
    att = lax.dot_general(q_in, k_in, (((1,), (1,)), ((), ())), preferred_element_type=F32)
    att = jnp.where(tril, att, 0.0)

    s_old = S[...]
    v = v_ref[...].astype(BF16)
    o = (jnp.dot(att.astype(BF16), v, preferred_element_type=F32)
         + jnp.dot(q_dec, s_old.astype(BF16), preferred_element_type=F32))

    b_end = b[valid - 1:valid, :]
    k_end = (k * jnp.exp(b_end - b)).astype(BF16)
    tn_dims = (((0,), (0,)), ((), ()))
    upd = lax.dot_general(k_end, v, tn_dims, preferred_element_type=F32)
    ones = jnp.ones((C, LANES), F32)
    b_col = (lax.dot_general(la_hi, ones, tn_dims, preferred_element_type=F32)
             + lax.dot_general(la_lo, ones, tn_dims, preferred_element_type=F32))
    dec = jnp.exp(b_col)
    S[...] = jnp.concatenate([dec] * (dv // LANES), axis=1) * s_old + upd

    gg = gg_ref[...].astype(F32)
    on = _rms(o, go_ref[...])
    o_ref[...] = (on * (gg * (1.0 / (1.0 + jnp.exp(-gg))))).astype(BF16)

    @pl.when(c == pl.num_programs(2) - 1)
    def _():
        sf_ref[...] = S[...]


def _gla(proj_a, glr, wgu, bg, go, l, *, M, n_seq, n_chunks, C, valid, row_blk0, in_row_blk0=None,
         s0=None, o_prev=None):
    H = GLA_HEADS
    dk = wgu.shape[2] // H
    dv = go.shape[2]
    has_s0 = s0 is not None
    if in_row_blk0 is None:
        in_row_blk0 = row_blk0

    def rb(b, c):
        return row_blk0 + b * n_chunks + c

    def rbi(b, c):
        return in_row_blk0 + b * n_chunks + c

    in_specs = [
        pl.BlockSpec((C, dk), lambda b, h, c: (rbi(b, c), h)),
        pl.BlockSpec((C, dk), lambda b, h, c: (rbi(b, c), H + h)),
        pl.BlockSpec((C, dv), lambda b, h, c: (rbi(b, c), (2 * H * dk) // dv + h)),
        pl.BlockSpec((C, dv), lambda b, h, c: (rbi(b, c), (2 * H * dk) // dv + H + h)),
        pl.BlockSpec((C, LANES), lambda b, h, c: (rb(b, c), 0)),
        pl.BlockSpec((None, LANES, dk), lambda b, h, c: (l, 0, h)),
        pl.BlockSpec((None, 1, dk), lambda b, h, c: (l, 0, h)),
        pl.BlockSpec((None, 1, dv), lambda b, h, c: (l, 0, 0)),
    ]
    args = [proj_a, proj_a, proj_a, proj_a, glr, wgu, bg, go]
    aliases = {}
    if has_s0:
        in_specs += [pl.BlockSpec((None, None, None, dk, dv), lambda b, h, c: (l, b, h, 0, 0)),
                     pl.BlockSpec(memory_space=pl.ANY)]
        args += [s0, o_prev]
        aliases = {len(args) - 1: 0}
    return pl.pallas_call(
        functools.partial(_gla_kernel, C=C, valid=valid, has_s0=has_s0, dk=dk, dv=dv),
        grid=(n_seq, H, n_chunks),
        in_specs=in_specs,
        out_specs=[pl.BlockSpec((C, dv), lambda b, h, c: (rb(b, c), h)),
                   pl.BlockSpec((None, None, dk, dv), lambda b, h, c: (b, h, 0, 0))],
        out_shape=[jax.ShapeDtypeStruct((M, H * dv), BF16),
                   jax.ShapeDtypeStruct((n_seq, H, dk, dv), F32)],
        scratch_shapes=[pltpu.VMEM((dk, dv), F32)],
        input_output_aliases=aliases,
        compiler_params=_cparams(),
        name="gla_sample" if has_s0 else "gla_prompt",
    )(*args)


def _head_norm(x, g):
    return x * lax.rsqrt(jnp.mean(x * x, axis=-1, keepdims=True) + EPS) * g


def _softmax_sink_pv(parts, sink):
    m = sink
    for s, _ in parts:
        m = jnp.maximum(m, jnp.max(s, axis=-1, keepdims=True))
    den = jnp.exp(sink - m)
    es = []
    for s, _ in parts:
        e = jnp.exp(s - m)
        den = den + jnp.sum(e, axis=-1, keepdims=True)
        es.append(e)
    inv = 1.0 / den
    acc = None
    for e, (_, v) in zip(es, parts):
        pv = jnp.dot((e * inv).astype(BF16), v, preferred_element_type=F32)
        acc = pv if acc is None else acc + pv
    return acc


NT_DIMS = (((1,), (1,)), ((), ()))


def _swa_prompt_kernel(sink_ref, q_ref, kp_ref, kc_ref, vp_ref, vc_ref, gq_ref, gk_ref,
                       o_ref, ko_ref, vo_ref, *, kvh):
    i = pl.program_id(1)
    hd = SWA_HEAD_DIM
    W = WINDOW
    r = lax.broadcasted_iota(jnp.int32, (W, 2 * W), 0)
    j = lax.broadcasted_iota(jnp.int32, (W, 2 * W), 1)
    mask = (j >= r) & (j <= r + W) & ((j >= W) | (i > 0))
    gq = gq_ref[...]
    gk = gk_ref[...]
    kp = kp_ref[...].astype(F32)
    kc = kc_ref[...].astype(F32)
    k_new = []
    outs = []
    for kv in range(kvh):
        sl = slice(kv * hd, (kv + 1) * hd)
        kcn = _head_norm(kc[:, sl], gk)
        k_new.append(kcn)
        kcat = jnp.concatenate([_head_norm(kp[:, sl], gk), kcn], axis=0).astype(BF16)
        vcat = jnp.concatenate([vp_ref[:, sl], vc_ref[:, sl]], axis=0)
        for g in range(SWA_GROUP):
            h = kv * SWA_GROUP + g
            qh = _head_norm(q_ref[:, h * hd:(h + 1) * hd].astype(F32), gq) * (hd ** -0.5)
            s = lax.dot_general(qh.astype(BF16), kcat, NT_DIMS, preferred_element_type=F32)
            s = jnp.where(mask, s, NEG_INF)
            outs.append(_softmax_sink_pv([(s, vcat)], sink_ref[h]))
    o_ref[...] = jnp.concatenate(outs, axis=1).astype(BF16)
    ko_ref[...] = jnp.concatenate(k_new, axis=1)
    vo_ref[...] = vc_ref[...].astype(F32)


def _swa_prompt(proj_b, sinks_l, gq, gk, l, *, B, T, kvh):
    M = proj_b.shape[0]
    hd = SWA_HEAD_DIM
    qw = kvh * SWA_GROUP * hd
    kw = kvh * hd
    nb = T // WINDOW
    kblk = qw // kw

    def cur(b, i, s):
        return b * nb + i

    def prev(b, i, s):
        return b * nb + jnp.maximum(i - 1, 0)

    grid_spec = pltpu.PrefetchScalarGridSpec(
        num_scalar_prefetch=1,
        grid=(B, nb),
        in_specs=[
            pl.BlockSpec((WINDOW, qw), lambda b, i, s: (cur(b, i, s), 0)),
            pl.BlockSpec((WINDOW, kw), lambda b, i, s: (prev(b, i, s), kblk)),
            pl.BlockSpec((WINDOW, kw), lambda b, i, s: (cur(b, i, s), kblk)),
            pl.BlockSpec((WINDOW, kw), lambda b, i, s: (prev(b, i, s), kblk + 1)),
            pl.BlockSpec((WINDOW, kw), lambda b, i, s: (cur(b, i, s), kblk + 1)),
            pl.BlockSpec((None, 1, hd), lambda b, i, s: (l, 0, 0)),
            pl.BlockSpec((None, 1, hd), lambda b, i, s: (l, 0, 0)),
        ],
        out_specs=[
            pl.BlockSpec((WINDOW, qw), lambda b, i, s: (cur(b, i, s), 0)),
            pl.BlockSpec((None, WINDOW, kw), lambda b, i, s: (b, 0, 0)),
            pl.BlockSpec((None, WINDOW, kw), lambda b, i, s: (b, 0, 0)),
        ],
    )
    return pl.pallas_call(
        functools.partial(_swa_prompt_kernel, kvh=kvh),
        grid_spec=grid_spec,
        out_shape=[jax.ShapeDtypeStruct((M, qw), BF16),
                   jax.ShapeDtypeStruct((B, WINDOW, kw), F32),
                   jax.ShapeDtypeStruct((B, WINDOW, kw), F32)],
        compiler_params=_cparams(),
        name="swa_prompt",
    )(sinks_l, proj_b, proj_b, proj_b, proj_b, proj_b, gq, gk)


def _swa_sample_kernel(sink_ref, q_ref, kn_ref, vn_ref, kc_ref, vc_ref, gq_ref, gk_ref, _o_prev,
                       o_ref, ko_ref, vo_ref, *, kvh, st, nbuf):
    hd = SWA_HEAD_DIM
    r1 = lax.broadcasted_iota(jnp.int32, (SPAD, nbuf), 0)
    j1 = lax.broadcasted_iota(jnp.int32, (SPAD, nbuf), 1)
    mask_buf = (nbuf + r1 - j1 >= 0) & (nbuf + r1 - j1 <= WINDOW)
    r2 = lax.broadcasted_iota(jnp.int32, (SPAD, SPAD), 0)
    t2 = lax.broadcasted_iota(jnp.int32, (SPAD, SPAD), 1)
    mask_new = (t2 <= r2) & (t2 < st)
    gq = gq_ref[...]
    gk = gk_ref[...]
    kn = kn_ref[...].astype(F32)
    k_new = []
    outs = []
    for kv in range(kvh):
        sl = slice(kv * hd, (kv + 1) * hd)
        knn = _head_norm(kn[:, sl], gk)
        k_new.append(knn)
        kbuf = kc_ref[:, sl].astype(BF16)
        vbuf = vc_ref[:, sl].astype(BF16)
        knb = knn.astype(BF16)
        vnb = vn_ref[:, sl].astype(BF16)
        for g in range(SWA_GROUP):
            h = kv * SWA_GROUP + g
            qh = (_head_norm(q_ref[:, h * hd:(h + 1) * hd].astype(F32), gq) * (hd ** -0.5)).astype(BF16)
            s1 = lax.dot_general(qh, kbuf, NT_DIMS, preferred_element_type=F32)
            s2 = lax.dot_general(qh, knb, NT_DIMS, preferred_element_type=F32)
            s1 = jnp.where(mask_buf, s1, NEG_INF)
            s2 = jnp.where(mask_new, s2, NEG_INF)
            outs.append(_softmax_sink_pv([(s1, vbuf), (s2, vnb)], sink_ref[h]))
    o_ref[...] = jnp.concatenate(outs, axis=1).astype(BF16)
    k_all = jnp.concatenate(k_new, axis=1)
    ko_ref[...] = jnp.concatenate([kc_ref[st:, :], k_all[:st, :]], axis=0)
    vo_ref[...] = jnp.concatenate([vc_ref[st:, :], vn_ref[:st, :].astype(F32)], axis=0)


def _swa_sample(proj_b, o_prev, cache_k, cache_v, sinks_l, gq, gk, l, *, SB, ST, kvh, row_blk0, in_row_blk0):
    M = o_prev.shape[0]
    hd = SWA_HEAD_DIM
    qw = kvh * SWA_GROUP * hd
    kw = kvh * hd
    nbuf = cache_k.shape[2]
    kblk = qw // kw
    grid_spec = pltpu.PrefetchScalarGridSpec(
        num_scalar_prefetch=1,
        grid=(SB,),
        in_specs=[
            pl.BlockSpec((SPAD, qw), lambda b, s: (in_row_blk0 + b, 0)),
            pl.BlockSpec((SPAD, kw), lambda b, s: (in_row_blk0 + b, kblk)),
            pl.BlockSpec((SPAD, kw), lambda b, s: (in_row_blk0 + b, kblk + 1)),
            pl.BlockSpec((None, None, nbuf, kw), lambda b, s: (l, b, 0, 0)),
            pl.BlockSpec((None, None, nbuf, kw), lambda b, s: (l, b, 0, 0)),
            pl.BlockSpec((None, 1, hd), lambda b, s: (l, 0, 0)),
            pl.BlockSpec((None, 1, hd), lambda b, s: (l, 0, 0)),
            pl.BlockSpec(memory_space=pl.ANY),
        ],
        out_specs=[
            pl.BlockSpec((SPAD, qw), lambda b, s: (row_blk0 + b, 0)),
            pl.BlockSpec((None, nbuf, kw), lambda b, s: (b, 0, 0)),
            pl.BlockSpec((None, nbuf, kw), lambda b, s: (b, 0, 0)),
        ],
    )
    return pl.pallas_call(
        functools.partial(_swa_sample_kernel, kvh=kvh, st=ST, nbuf=nbuf),
        grid_spec=grid_spec,
        out_shape=[jax.ShapeDtypeStruct((M, qw), BF16),
                   jax.ShapeDtypeStruct((SB, nbuf, kw), F32),
                   jax.ShapeDtypeStruct((SB, nbuf, kw), F32)],
        input_output_aliases={8: 0},
        compiler_params=_cparams(),
        name="swa_sample",
    )(sinks_l, proj_b, proj_b, proj_b, cache_k, cache_v, gq, gk, o_prev)


def _post_attn_kernel(x_ref, mix_ref, g_ref, gtp_ref, gts_ref, shp_ref, shs_ref, scp_ref, scs_ref,
                      wr_ref, x1_ref, h2_ref, lg_ref, *, npb):
    i = pl.program_id(0)
    gt = _pick_mod(i, npb, gtp_ref, gts_ref)
    x1 = x_ref[...] + gt * mix_ref[...]
    x1_ref[...] = x1
    sh = _pick_mod(i, npb, shp_ref, shs_ref)
    sc = _pick_mod(i, npb, scp_ref, scs_ref)
    h2 = _rms(x1, g_ref[...]) * (1.0 + sc) + sh
    hb = h2.astype(BF16)
    h2_ref[...] = hb
    lg_ref[...] = jnp.dot(hb, wr_ref[...], preferred_element_type=F32)


def _post_attn(rows, x, mix, g, l, modp, mods, wr):
    M, D = x.shape
    gtp, gts = _mod_specs(rows, D, 2)
    shp, shs = _mod_specs(rows, D, 3)
    scp, scs = _mod_specs(rows, D, 4)
    row_spec = pl.BlockSpec((ROW_BLK, D), lambda i: (i, 0))
    wr_spec = pl.BlockSpec((D, LANES), lambda i: (0, 0))
    return pl.pallas_call(
        functools.partial(_post_attn_kernel, npb=rows.npb),
        grid=(rows.nblk,),
        in_specs=[row_spec, row_spec, pl.BlockSpec((None, 1, D), lambda i: (l, 0, 0)),
                  gtp, gts, shp, shs, scp, scs, wr_spec],
        out_specs=[row_spec, row_spec, pl.BlockSpec((ROW_BLK, LANES), lambda i: (i, 0))],
        out_shape=[jax.ShapeDtypeStruct((M, D), F32),
                   jax.ShapeDtypeStruct((M, D), BF16),
                   jax.ShapeDtypeStruct((M, LANES), F32)],
        compiler_params=_cparams(),
        name="post_attn",
    )(x, mix, g, modp, mods, modp, mods, modp, mods, wr)


def _first_block_of_expert(be_ref, i):
    return (i == 0) | (be_ref[i] != be_ref[jnp.maximum(i - 1, 0)])


def _moe_up_kernel(be_ref, na_ref, x_ref, wg_ref, wu_ref, h_ref, wgb, wub):
    i = pl.program_id(1)

    @pl.when(_first_block_of_expert(be_ref, i))
    def _():
        wgb[...] = wg_ref[...].astype(BF16)
        wub[...] = wu_ref[...].astype(BF16)

    @pl.when(i < na_ref[0])
    def _():
        x = x_ref[...]
        a = jnp.dot(x, wgb[...], preferred_element_type=F32)
        u = jnp.dot(x, wub[...], preferred_element_type=F32)
        h_ref[...] = (a * (1.0 / (1.0 + jnp.exp(-a))) * u).astype(BF16)

    @pl.when(i >= na_ref[0])
    def _():
        h_ref[...] = jnp.zeros_like(h_ref)


def _moe_down_kernel(be_ref, na_ref, h_ref, wd_ref, y_ref, wdb):
    i = pl.program_id(1)

    @pl.when(_first_block_of_expert(be_ref, i))
    def _():
        wdb[...] = wd_ref[...].astype(BF16)

    @pl.when(i < na_ref[0])
    def _():
        y_ref[...] = jnp.dot(h_ref[...], wdb[...], preferred_element_type=F32)

    @pl.when(i >= na_ref[0])
    def _():
        y_ref[...] = jnp.zeros_like(y_ref)


def _moe_ffn(xs, block_e, n_active, w_g, w_u, w_d, l):
    P, D = xs.shape
    F = w_g.shape[3]
    nb = P // MOE_BLK
    up_spec = pltpu.PrefetchScalarGridSpec(
        num_scalar_prefetch=2,
        grid=(F // MOE_FC, nb),
        in_specs=[pl.BlockSpec((MOE_BLK, D), lambda j, i, be, na: (i, 0)),
                  pl.BlockSpec((None, None, D, MOE_FC), lambda j, i, be, na: (l, be[i], 0, j)),
                  pl.BlockSpec((None, None, D, MOE_FC), lambda j, i, be, na: (l, be[i], 0, j))],
        out_specs=pl.BlockSpec((MOE_BLK, MOE_FC), lambda j, i, be, na: (i, j)),
        scratch_shapes=[pltpu.VMEM((D, MOE_FC), BF16), pltpu.VMEM((D, MOE_FC), BF16)],
    )
    hmid = pl.pallas_call(
        _moe_up_kernel, grid_spec=up_spec,
        out_shape=jax.ShapeDtypeStruct((P, F), BF16),
        compiler_params=_cparams(), name="moe_up",
    )(block_e, n_active, xs, w_g, w_u)
    down_spec = pltpu.PrefetchScalarGridSpec(
        num_scalar_prefetch=2,
        grid=(D // MOE_DC, nb),
        in_specs=[pl.BlockSpec((MOE_BLK, F), lambda j, i, be, na: (i, 0)),
                  pl.BlockSpec((None, None, F, MOE_DC), lambda j, i, be, na: (l, be[i], 0, j))],
        out_specs=pl.BlockSpec((MOE_BLK, MOE_DC), lambda j, i, be, na: (i, j)),
        scratch_shapes=[pltpu.VMEM((F, MOE_DC), BF16)],
    )
    return pl.pallas_call(
        _moe_down_kernel, grid_spec=down_spec,
        out_shape=jax.ShapeDtypeStruct((P, D), F32),
        compiler_params=_cparams(), name="moe_down",
    )(block_e, n_active, hmid, w_d)


def _route(logits, b_router, valid_row):
    E = b_router.shape[0]
    probs = jax.nn.softmax(logits + b_router.astype(F32), axis=-1)
    pg = probs.reshape(-1, N_GROUPS, E // N_GROUPS)
    group_score = jnp.sum(lax.top_k(pg, TOP_K)[0], axis=-1)
    g_sel = jnp.argmax(group_score, axis=-1)
    in_group = jnp.take_along_axis(pg, g_sel[:, None, None], axis=1)[:, 0]
    vals, idx = lax.top_k(in_group, TOP_K)
    expert_idx = (g_sel[:, None] * (E // N_GROUPS) + idx).astype(jnp.int32)
    gates = vals / jnp.sum(vals, axis=-1, keepdims=True)
    expert_idx = jnp.where(valid_row[:, None], expert_idx, E)
    gates = jnp.where(valid_row[:, None], gates, 0.0)
    return expert_idx, gates


def _dispatch(expert_idx, n_valid, E):
    M = expert_idx.shape[0]
    MK = M * TOP_K
    flat_e = expert_idx.reshape(MK)
    flat_tok = jnp.repeat(jnp.arange(M, dtype=jnp.int32), TOP_K)
    order = jnp.argsort(flat_e)
    se = flat_e[order]
    counts = jnp.sum(flat_e[:, None] == jnp.arange(E, dtype=jnp.int32)[None, :], axis=0).astype(jnp.int32)
    padded = (counts + MOE_BLK - 1) // MOE_BLK * MOE_BLK
    pad_end = jnp.cumsum(padded)
    pad_start = pad_end - padded
    start = jnp.cumsum(counts) - counts
    nb = (n_valid * TOP_K + E * (MOE_BLK - 1)) // MOE_BLK + 1
    P = nb * MOE_BLK
    se_c = jnp.minimum(se, E - 1)
    dest = pad_start[se_c] + jnp.arange(MK, dtype=jnp.int32) - start[se_c]
    dest = jnp.where(se < E, dest, P)
    slot_tok = jnp.full((P,), M, jnp.int32).at[dest].set(flat_tok[order], mode="drop")
    block_e = jnp.minimum(jnp.searchsorted(pad_end, jnp.arange(nb, dtype=jnp.int32) * MOE_BLK, side="right"),
                          E - 1).astype(jnp.int32)
    n_active = (pad_end[-1:] // MOE_BLK).astype(jnp.int32)
    pair_dest = jnp.zeros((MK,), jnp.int32).at[order].set(jnp.minimum(dest, P - 1)).reshape(M, TOP_K)
    return slot_tok, block_e, n_active, pair_dest


def _post_moe_kernel(x_ref, y0_ref, y1_ref, gw_ref, gtp_ref, gts_ref, *rest, npb, with_norm):
    i = pl.program_id(0)
    gt = _pick_mod(i, npb, gtp_ref, gts_ref)
    gw = gw_ref[...]
    y = gw[:, 0:1] * y0_ref[...].astype(F32) + gw[:, 1:2] * y1_ref[...].astype(F32)
    x2 = x_ref[...] + gt * y
    if with_norm:
        g_ref, shp_ref, shs_ref, scp_ref, scs_ref, x2_ref, h_ref = rest
        x2_ref[...] = x2
        sh = _pick_mod(i, npb, shp_ref, shs_ref)
        sc = _pick_mod(i, npb, scp_ref, scs_ref)
        h_ref[...] = (_rms(x2, g_ref[...]) * (1.0 + sc) + sh).astype(BF16)
    else:
        (x2_ref,) = rest
        x2_ref[...] = x2


def _post_moe(rows, x1, y0, y1, gates_pad, l, modp, mods, g_next=None, modp_next=None, mods_next=None):
    M, D = x1.shape
    with_norm = g_next is not None
    gtp, gts = _mod_specs(rows, D, 5)
    row_spec = pl.BlockSpec((ROW_BLK, D), lambda i: (i, 0))
    in_specs = [row_spec, row_spec, row_spec, pl.BlockSpec((ROW_BLK, LANES), lambda i: (i, 0)), gtp, gts]
    args = [x1, y0, y1, gates_pad, modp, mods]
    out_specs = [row_spec]
    out_shape = [jax.ShapeDtypeStruct((M, D), F32)]
    if with_norm:
        shp, shs = _mod_specs(rows, D, 0)
        scp, scs = _mod_specs(rows, D, 1)
        in_specs += [pl.BlockSpec((None, 1, D), lambda i: (l + 1, 0, 0)), shp, shs, scp, scs]
        args += [g_next, modp_next, mods_next, modp_next, mods_next]
        out_specs.append(row_spec)
        out_shape.append(jax.ShapeDtypeStruct((M, D), BF16))
    return pl.pallas_call(
        functools.partial(_post_moe_kernel, npb=rows.npb, with_norm=with_norm),
        grid=(rows.nblk,),
        in_specs=in_specs, out_specs=out_specs, out_shape=out_shape,
        compiler_params=_cparams(),
        name="post_moe",
    )(*args)


def kernel(x_prompt, x_sample, c_prompt, c_sample, state_gla, cache_swa_k, cache_swa_v, w_ada, b_ada, g_norm1, g_norm2, w_in, w_gate_up, b_gate, g_gla_out, g_q_norm, g_k_norm, sinks, w_out, w_router, b_router, w_e_gate, w_e_up, w_e_down):
    B, T, D = x_prompt.shape
    SB, ST, _ = x_sample.shape
    L = w_ada.shape[0]
    H = GLA_HEADS
    dk = state_gla.shape[3]
    dv = state_gla.shape[4]
    nbuf, kvh, hd = cache_swa_k.shape[2:]
    E = w_router.shape[1]
    rows = _Rows(B, T, SB, ST)
    M = rows.M

    gla_cols = 2 * H * dk + 2 * H * dv
    swa_col0 = gla_cols + GATE_RANK
    swa_cols = w_in.shape[2] - swa_col0

    xs_pad = jnp.pad(x_sample, ((0, 0), (0, SPAD - ST), (0, 0))).reshape(SB * SPAD, D)
    x = jnp.concatenate([x_prompt.reshape(B * T, D), xs_pad], axis=0)
    row_id = jnp.arange(M, dtype=jnp.int32)
    valid_row = (row_id < rows.n_prompt) | ((row_id - rows.n_prompt) % SPAD < ST)
    n_valid = rows.n_prompt + SB * ST

    mod = _ada_mod(jnp.concatenate([c_prompt, c_sample], axis=0), w_ada, b_ada)
    modp_all = mod[:, :B].reshape(L, B, 1, 6 * D)
    mods_all = jnp.repeat(mod[:, B:], SPAD, axis=1)

    wgu_pad = jnp.pad(w_gate_up, ((0, 0), (0, LANES - GATE_RANK), (0, 0))).astype(BF16)
    bg3 = b_gate.reshape(L, 1, H * dk)
    go3 = g_gla_out.reshape(L, 1, dv)
    gq3 = g_q_norm.reshape(L, 1, hd)
    gk3 = g_k_norm.reshape(L, 1, hd)
    g1 = g_norm1.reshape(L, 1, D)
    g2 = g_norm2.reshape(L, 1, D)
    wr = jnp.pad(w_router, ((0, 0), (0, LANES - E))).astype(BF16)
    tm = MM_ROWS
    tail_blk0 = (rows.n_prompt - (M - tm)) // SPAD
    assert M % tm == 0 and M - tm <= rows.n_prompt and (rows.n_prompt - (M - tm)) % SPAD == 0
    ck =cache_swa_k.reshape(L, SB, nbuf, kvh * hd)
    cv = cache_swa_v.reshape(L, SB, nbuf, kvh * hd)

    h = _norm_mod(rows, x, g1, 0, modp_all[0], mods_all[0])
    sp_l, kp_l, vp_l, ss_l, ks_l, vs_l = [], [], [], [], [], []
    for l in range(L):
        modp, mods = modp_all[l], mods_all[l]
        proj_a, proj_a_s = _matmul([h], w_in, l, 0, gla_cols, 512, tm, BF16, "proj_gla", with_tail=True)
        (glr,) = _matmul([h], w_in, l, gla_cols // LANES, LANES, LANES, tm, BF16, "proj_gate_lr")
        w_swa = w_in[l, :, swa_col0:].astype(BF16)
        proj_b, proj_b_s = _matmul([h], w_swa, None, 0, swa_cols, 512, tm, BF16, "proj_swa", with_tail=True)

        o_gla, s_p = _gla(proj_a, glr, wgu_pad, bg3, go3, l, M=M, n_seq=B, n_chunks=T // GLA_CHUNK,
                          C=GLA_CHUNK, valid=GLA_CHUNK, row_blk0=0)
        o_gla, s_s = _gla(proj_a_s, glr, wgu_pad, bg3, go3, l, M=M, n_seq=SB, n_chunks=1, C=SPAD, valid=ST,
                          row_blk0=rows.n_prompt // SPAD, in_row_blk0=tail_blk0, s0=state_gla, o_prev=o_gla)
        o_swa, k_p, v_p = _swa_prompt(proj_b, sinks[l], gq3, gk3, l, B=B, T=T, kvh=kvh)
        o_swa, k_s, v_s = _swa_sample(proj_b_s, o_swa, ck, cv, sinks[l], gq3, gk3, l, SB=SB, ST=ST, kvh=kvh,
                                      row_blk0=rows.n_prompt // SPAD, in_row_blk0=tail_blk0)

        (mix,) = _matmul([o_gla, o_swa], w_out, l, 0, D, 512, tm, F32, "out_proj")
        x1, h2, logits = _post_attn(rows, x, mix, g2, l, modp, mods, wr)

        expert_idx, gates = _route(logits[:, :E], b_router, valid_row)
        slot_tok, block_e, n_active, pair_dest = _dispatch(expert_idx, n_valid, E)
        h2_ext = jnp.concatenate([h2, jnp.zeros((1, D), BF16)], axis=0)
        ys = _moe_ffn(h2_ext[slot_tok], block_e, n_active, w_e_gate, w_e_up, w_e_down, l)
        y0 = ys[pair_dest[:, 0]]
        y1 = ys[pair_dest[:, 1]]
        gates_pad = jnp.pad(gates, ((0, 0), (0, LANES - TOP_K)))
        if l + 1 < L:
            x, h = _post_moe(rows, x1, y0, y1, gates_pad, l, modp, mods, g1, modp_all[l + 1], mods_all[l + 1])
        else:
            (x,) = _post_moe(rows, x1, y0, y1, gates_pad, l, modp, mods)

        sp_l.append(s_p)
        ss_l.append(s_s)
        kp_l.append(k_p.reshape(B, WINDOW, kvh, hd))
        vp_l.append(v_p.reshape(B, WINDOW, kvh, hd))
        ks_l.append(k_s.reshape(SB, nbuf, kvh, hd))
        vs_l.append(v_s.reshape(SB, nbuf, kvh, hd))

    y_prompt = x[:rows.n_prompt].reshape(B, T, D)
    y_sample = x[rows.n_prompt:].reshape(SB, SPAD, D)[:, :ST]
    return (y_prompt, y_sample, jnp.stack(sp_l), jnp.stack(kp_l), jnp.stack(vp_l),
            jnp.stack(ss_l), jnp.stack(ks_l), jnp.stack(vs_l))
```

```python
import functools

import jax
import jax.numpy as jnp
from jax import lax
from jax.experimental import pallas as pl
from jax.experimental.pallas import tpu as pltpu

F32 = jnp.float32
BF16 = jnp.bfloat16

GLA_HEADS = 4
GATE_RANK = 16
GATE_TEMP = 16.0
SWA_HEAD_DIM = 64
SWA_GROUP = 8
WINDOW = 128
N_GROUPS = 4
TOP_K = 2
EPS = 1e-6
NEG_INF = -1e30

LANES = 128
BF16_SUBLANES = 16
VMEM_LIMIT = 48 * 1024 * 1024

SPAD = BF16_SUBLANES
ROW_BLK = 128
GLA_CHUNK = 128
MM_ROWS = 1088
MOE_BLK = 256
MOE_FC = 512
MOE_DC = 2048


def _cparams(**kw):
    return pltpu.CompilerParams(vmem_limit_bytes=VMEM_LIMIT, **kw)


def _ada_kernel(c_ref, w_ref, b_ref, o_ref):
    c = c_ref[...]
    a = (c * (1.0 / (1.0 + jnp.exp(-c)))).astype(BF16)
    o_ref[...] = jnp.dot(a, w_ref[...].astype(BF16), preferred_element_type=F32) + b_ref[...]


def _ada_mod(c_all, w_ada, b_ada):
    L, D, N = w_ada.shape
    R = c_all.shape[0]
    tn = 512
    return pl.pallas_call(
        _ada_kernel,
        grid=(L, N // tn),
        in_specs=[pl.BlockSpec((R, D), lambda l, j: (0, 0)),
                  pl.BlockSpec((None, D, tn), lambda l, j: (l, 0, j)),
                  pl.BlockSpec((None, 1, tn), lambda l, j: (l, 0, j))],
        out_specs=pl.BlockSpec((None, R, tn), lambda l, j: (l, 0, j)),
        out_shape=jax.ShapeDtypeStruct((L, R, N), F32),
        compiler_params=_cparams(),
        name="ada_mod",
    )(c_all, w_ada, b_ada.reshape(L, 1, N))


class _Rows:
    def __init__(self, B, T, SB, ST):
        self.B, self.T, self.SB, self.ST = B, T, SB, ST
        self.n_prompt = B * T
        self.n_sample = SB * SPAD
        self.M = self.n_prompt + self.n_sample
        assert T % ROW_BLK == 0 and self.n_sample % ROW_BLK == 0 and ST <= SPAD
        self.npb = self.n_prompt // ROW_BLK
        self.nsb = self.n_sample // ROW_BLK
        self.nblk = self.npb + self.nsb
        self.blk_per_seq = T // ROW_BLK


def _mod_specs(rows, D, col):
    npb, bps, B = rows.npb, rows.blk_per_seq, rows.B
    p_spec = pl.BlockSpec((None, 1, D), lambda i: (jnp.minimum(i // bps, B - 1), 0, col))
    s_spec = pl.BlockSpec((ROW_BLK, D), lambda i: (jnp.maximum(i - npb, 0), col))
    return p_spec, s_spec


def _pick_mod(i, npb, p_ref, s_ref):
    return jnp.where(i < npb, p_ref[...], s_ref[...])


def _rms(x, g):
    return x * lax.rsqrt(jnp.mean(x * x, axis=-1, keepdims=True) + EPS) * g


def _norm_mod_kernel(x_ref, g_ref, shp_ref, shs_ref, scp_ref, scs_ref, h_ref, *, npb):
    i = pl.program_id(0)
    y = _rms(x_ref[...], g_ref[...])
    sh = _pick_mod(i, npb, shp_ref, shs_ref)
    sc = _pick_mod(i, npb, scp_ref, scs_ref)
    h_ref[...] = (y * (1.0 + sc) + sh).astype(BF16)


def _norm_mod(rows, x, g, l, modp, mods):
    M, D = x.shape
    shp, shs = _mod_specs(rows, D, 0)
    scp, scs = _mod_specs(rows, D, 1)
    row_spec = pl.BlockSpec((ROW_BLK, D), lambda i: (i, 0))
    return pl.pallas_call(
        functools.partial(_norm_mod_kernel, npb=rows.npb),
        grid=(rows.nblk,),
        in_specs=[row_spec, pl.BlockSpec((None, 1, D), lambda i: (l, 0, 0)), shp, shs, scp, scs],
        out_specs=row_spec,
        out_shape=jax.ShapeDtypeStruct((M, D), BF16),
        compiler_params=_cparams(),
        name="norm_mod",
    )(x, g, modp, mods, modp, mods)


def _mm_kernel(*refs, n_lhs, with_tail):
    x_refs, w_ref, o_ref = refs[:n_lhs], refs[n_lhs], refs[n_lhs + 1]
    wbf_ref = refs[-1]

    @pl.when(pl.program_id(1) == 0)
    def _():
        wbf_ref[...] = w_ref[...].astype(BF16)

    acc = None
    k0 = 0
    for x_ref in x_refs:
        kk = x_ref.shape[1]
        part = jnp.dot(x_ref[...], wbf_ref[k0:k0 + kk, :], preferred_element_type=F32)
        acc = part if acc is None else acc + part
        k0 += kk
    o_ref[...] = acc.astype(o_ref.dtype)
    if with_tail:
        @pl.when(pl.program_id(1) == pl.num_programs(1) - 1)
        def _():
            refs[n_lhs + 2][...] = acc


def _matmul(xs, w, w_lead, col_blk0, n_cols, tn, tm, out_dtype, name, with_tail=False):
    M = xs[0].shape[0]
    K = sum(x.shape[1] for x in xs)
    if w.ndim == 3:
        w_spec = pl.BlockSpec((None, K, tn), lambda j, i: (w_lead, 0, col_blk0 + j))
    else:
        w_spec = pl.BlockSpec((K, tn), lambda j, i: (0, col_blk0 + j))
    x_specs = [pl.BlockSpec((tm, x.shape[1]), lambda j, i: (i, 0)) for x in xs]
    out_specs = [pl.BlockSpec((tm, tn), lambda j, i: (i, j))]
    out_shape = [jax.ShapeDtypeStruct((M, n_cols), out_dtype)]
    if with_tail:
        out_specs.append(pl.BlockSpec((tm, tn), lambda j, i: (0, j)))
        out_shape.append(jax.ShapeDtypeStruct((tm, n_cols), F32))
    return pl.pallas_call(
        functools.partial(_mm_kernel, n_lhs=len(xs), with_tail=with_tail),
        grid=(n_cols // tn, M // tm),
        in_specs=x_specs + [w_spec],
        out_specs=out_specs,
        out_shape=out_shape,
        scratch_shapes=[pltpu.VMEM((K, tn), BF16)],
        compiler_params=_cparams(),
        name=name,
    )(*xs, w)


def _gla_kernel(*refs, C, valid, has_s0, dk, dv):
    q_ref, k_ref, v_ref, gg_ref, glr_ref, wgu_ref, bg_ref, go_ref = refs[:8]
    s0_ref = refs[8] if has_s0 else None
    o_ref, sf_ref, S = refs[-3:]
    c = pl.program_id(2)

    @pl.when(c == 0)
    def _():
        if has_s0:
            S[...] = s0_ref[...]
        else:
            S[...] = jnp.zeros_like(S)

    z = jnp.dot(glr_ref[...], wgu_ref[...], preferred_element_type=F32) + bg_ref[...]
    la = (jnp.minimum(z, 0.0) - jnp.log(1.0 + jnp.exp(-jnp.abs(z)))) * (1.0 / GATE_TEMP)
    q = q_ref[...].astype(F32) * (dk ** -0.5)
    k = k_ref[...].astype(F32)
    if valid < C:
        live = lax.broadcasted_iota(jnp.int32, (C, 1), 0) < valid
        la = jnp.where(live, la, 0.0)
        k = jnp.where(live, k, 0.0)

    la_hi = la.astype(BF16).astype(F32)
    la_lo = (la - la_hi).astype(BF16).astype(F32)
    ri = lax.broadcasted_iota(jnp.int32, (C, C), 0)
    ci = lax.broadcasted_iota(jnp.int32, (C, C), 1)
    tril = ri >= ci
    trilf = tril.astype(F32)
    b = (jnp.dot(trilf, la_hi, preferred_element_type=F32)
         + jnp.dot(trilf, la_lo, preferred_element_type=F32))

    q_dec = (q * jnp.exp(b)).astype(BF16)
    if C > SPAD:
        r = b[C // 2:C // 2 + 1, :]
        q_in = (q * jnp.exp(b - r)).astype(BF16)
        k_in = (k * jnp.exp(r - b)).astype(BF16)
    else:
        q_in = q_dec
        k_in = (k * jnp.exp(-b)).astype(BF16)
    att = lax.dot_general(q_in, k_in, (((1,), (1,)), ((), ())), preferred_element_type=F32)
    att = jnp.where(tril, att, 0.0)

    s_old = S[...]
    v = v_ref[...].astype(BF16)
    o = (jnp.dot(att.astype(BF16), v, preferred_element_type=F32)
         + jnp.dot(q_dec, s_old.astype(BF16), preferred_element_type=F32))

    b_end = b[valid - 1:valid, :]
    k_end = (k * jnp.exp(b_end - b)).astype(BF16)
    tn_dims = (((0,), (0,)), ((), ()))
    upd = lax.dot_general(k_end, v, tn_dims, preferred_element_type=F32)
    ones = jnp.ones((C, LANES), F32)
    b_col = (lax.dot_general(la_hi, ones, tn_dims, preferred_element_type=F32)
             + lax.dot_general(la_lo, ones, tn_dims, preferred_element_type=F32))
    dec = jnp.exp(b_col)
    S[...] = jnp.concatenate([dec] * (dv // LANES), axis=1) * s_old + upd

    gg = gg_ref[...].astype(F32)
    on = _rms(o, go_ref[...])
    o_ref[...] = (on * (gg * (1.0 / (1.0 + jnp.exp(-gg))))).astype(BF16)

    @pl.when(c == pl.num_programs(2) - 1)
    def _():
        sf_ref[...] = S[...]


def _gla(proj_a, glr, wgu, bg, go, l, *, M, L, n_seq, n_chunks, C, valid, row_blk0, in_row_blk0=None,
         s0=None, o_prev=None, stack=None):
    H = GLA_HEADS
    dk = wgu.shape[2] // H
    dv = go.shape[2]
    has_s0 = s0 is not None
    if in_row_blk0 is None:
        in_row_blk0 = row_blk0

    def rb(b, c):
        return row_blk0 + b * n_chunks + c

    def rbi(b, c):
        return in_row_blk0 + b * n_chunks + c

    in_specs = [
        pl.BlockSpec((C, dk), lambda b, h, c: (rbi(b, c), h)),
        pl.BlockSpec((C, dk), lambda b, h, c: (rbi(b, c), H + h)),
        pl.BlockSpec((C, dv), lambda b, h, c: (rbi(b, c), (2 * H * dk) // dv + h)),
        pl.BlockSpec((C, dv), lambda b, h, c: (rbi(b, c), (2 * H * dk) // dv + H + h)),
        pl.BlockSpec((C, LANES), lambda b, h, c: (rb(b, c), 0)),
        pl.BlockSpec((None, LANES, dk), lambda b, h, c: (l, 0, h)),
        pl.BlockSpec((None, 1, dk), lambda b, h, c: (l, 0, h)),
        pl.BlockSpec((None, 1, dv), lambda b, h, c: (l, 0, 0)),
    ]
    args = [proj_a, proj_a, proj_a, proj_a, glr, wgu, bg, go]
    aliases = {}
    if has_s0:
        in_specs += [pl.BlockSpec((None, None, None, dk, dv), lambda b, h, c: (l, b, h, 0, 0)),
                     pl.BlockSpec(memory_space=pl.ANY)]
        args += [s0, o_prev]
        aliases = {len(args) - 1: 0}
    if stack is not None:
        in_specs.append(pl.BlockSpec(memory_space=pl.ANY))
        args.append(stack)
        aliases[len(args) - 1] = 1
    return pl.pallas_call(
        functools.partial(_gla_kernel, C=C, valid=valid, has_s0=has_s0, dk=dk, dv=dv),
        grid=(n_seq, H, n_chunks),
        in_specs=in_specs,
        out_specs=[pl.BlockSpec((C, dv), lambda b, h, c: (rb(b, c), h)),
                   pl.BlockSpec((None, None, None, dk, dv), lambda b, h, c: (l, b, h, 0, 0))],
        out_shape=[jax.ShapeDtypeStruct((M, H * dv), BF16),
                   jax.ShapeDtypeStruct((L, n_seq, H, dk, dv), F32)],
        scratch_shapes=[pltpu.VMEM((dk, dv), F32)],
        input_output_aliases=aliases,
        compiler_params=_cparams(),
        name="gla_sample" if has_s0 else "gla_prompt",
    )(*args)


def _head_norm(x, g):
    return x * lax.rsqrt(jnp.mean(x * x, axis=-1, keepdims=True) + EPS) * g


def _segment_rms_inv(x, seg):
    R, Wd = x.shape
    assert Wd // seg <= LANES
    sel = (lax.broadcasted_iota(jnp.int32, (Wd, LANES), 0) // seg
           == lax.broadcasted_iota(jnp.int32, (Wd, LANES), 1)).astype(F32).astype(BF16)
    sel_t = (lax.broadcasted_iota(jnp.int32, (LANES, Wd), 0)
             == lax.broadcasted_iota(jnp.int32, (LANES, Wd), 1) // seg).astype(F32).astype(BF16)

    def split(a):
        hi = a.astype(BF16)
        return hi, (a - hi.astype(F32)).astype(BF16)

    sq_hi, sq_lo = split(x * x)
    sums = (jnp.dot(sq_hi, sel, preferred_element_type=F32)
            + jnp.dot(sq_lo, sel, preferred_element_type=F32))
    inv_hi, inv_lo = split(lax.rsqrt(sums * (1.0 / seg) + EPS))
    return (jnp.dot(inv_hi, sel_t, preferred_element_type=F32)
            + jnp.dot(inv_lo, sel_t, preferred_element_type=F32))


def _softmax_sink_pv(parts, sink):
    m = sink
    for s, _ in parts:
        m = jnp.maximum(m, jnp.max(s, axis=-1, keepdims=True))
    den = jnp.exp(sink - m)
    es = []
    for s, _ in parts:
        e = jnp.exp(s - m)
        den = den + jnp.sum(e, axis=-1, keepdims=True)
        es.append(e)
    inv = 1.0 / den
    acc = None
    for e, (_, v) in zip(es, parts):
        pv = jnp.dot((e * inv).astype(BF16), v, preferred_element_type=F32)
        acc = pv if acc is None else acc + pv
    return acc


NT_DIMS = (((1,), (1,)), ((), ()))


def _swa_prompt_kernel(sink_ref, q_ref, kp_ref, kc_ref, vp_ref, vc_ref, gq_ref, gk_ref,
                       o_ref, ko_ref, vo_ref, *, kvh):
    i = pl.program_id(1)
    hd = SWA_HEAD_DIM
    W = WINDOW
    G = SWA_GROUP
    r = lax.broadcasted_iota(jnp.int32, (G * W, 2 * W), 0) % W
    j = lax.broadcasted_iota(jnp.int32, (G * W, 2 * W), 1)
    mask = (j >= r) & (j <= r + W) & ((j >= W) | (i > 0))
    head_of_row = lax.broadcasted_iota(jnp.int32, (G * W, 1), 0) // W
    gk = gk_ref[...]
    q = q_ref[...].astype(F32)
    gq_row = jnp.concatenate([gq_ref[...]] * (q.shape[1] // hd), axis=1)
    qn = (q * _segment_rms_inv(q, hd) * gq_row * (hd ** -0.5)).astype(BF16)
    kp = kp_ref[...].astype(F32)
    kc = kc_ref[...].astype(F32)
    k_new = []
    outs = []
    for kv in range(kvh):
        sl = slice(kv * hd, (kv + 1) * hd)
        kcn = _head_norm(kc[:, sl], gk)
        k_new.append(kcn)
        kcat = jnp.concatenate([_head_norm(kp[:, sl], gk), kcn], axis=0).astype(BF16)
        vcat = jnp.concatenate([vp_ref[:, sl], vc_ref[:, sl]], axis=0)
        h0 = kv * G
        qg = jnp.concatenate([qn[:, (h0 + g) * hd:(h0 + g + 1) * hd] for g in range(G)], axis=0)
        sink = jnp.zeros((G * W, 1), F32)
        for g in range(G):
            sink = jnp.where(head_of_row == g, sink_ref[h0 + g], sink)
        s = lax.dot_general(qg, kcat, NT_DIMS, preferred_element_type=F32)
        s = jnp.where(mask, s, NEG_INF)
        og = _softmax_sink_pv([(s, vcat)], sink)
        outs += [og[g * W:(g + 1) * W, :] for g in range(G)]
    o_ref[...] = jnp.concatenate(outs, axis=1).astype(BF16)
    ko_ref[...] = jnp.concatenate(k_new, axis=1)
    vo_ref[...] = vc_ref[...].astype(F32)


def _swa_prompt(proj_b, sinks_l, gq, gk, l, *, B, T, kvh):
    M = proj_b.shape[0]
    hd = SWA_HEAD_DIM
    qw = kvh * SWA_GROUP * hd
    kw = kvh * hd
    nb = T // WINDOW
    kblk = qw // kw

    def cur(b, i, s):
        return b * nb + i

    def prev(b, i, s):
        return b * nb + jnp.maximum(i - 1, 0)

    grid_spec = pltpu.PrefetchScalarGridSpec(
        num_scalar_prefetch=1,
        grid=(B, nb),
        in_specs=[
            pl.BlockSpec((WINDOW, qw), lambda b, i, s: (cur(b, i, s), 0)),
            pl.BlockSpec((WINDOW, kw), lambda b, i, s: (prev(b, i, s), kblk)),
            pl.BlockSpec((WINDOW, kw), lambda b, i, s: (cur(b, i, s), kblk)),
            pl.BlockSpec((WINDOW, kw), lambda b, i, s: (prev(b, i, s), kblk + 1)),
            pl.BlockSpec((WINDOW, kw), lambda b, i, s: (cur(b, i, s), kblk + 1)),
            pl.BlockSpec((None, 1, hd), lambda b, i, s: (l, 0, 0)),
            pl.BlockSpec((None, 1, hd), lambda b, i, s: (l, 0, 0)),
        ],
        out_specs=[
            pl.BlockSpec((WINDOW, qw), lambda b, i, s: (cur(b, i, s), 0)),
            pl.BlockSpec((None, WINDOW, kw), lambda b, i, s: (b, 0, 0)),
            pl.BlockSpec((None, WINDOW, kw), lambda b, i, s: (b, 0, 0)),
        ],
    )
    return pl.pallas_call(
        functools.partial(_swa_prompt_kernel, kvh=kvh),
        grid_spec=grid_spec,
        out_shape=[jax.ShapeDtypeStruct((M, qw), BF16),
                   jax.ShapeDtypeStruct((B, WINDOW, kw), F32),
                   jax.ShapeDtypeStruct((B, WINDOW, kw), F32)],
        compiler_params=_cparams(),
        name="swa_prompt",
    )(sinks_l, proj_b, proj_b, proj_b, proj_b, proj_b, gq, gk)


def _swa_sample_kernel(sink_ref, q_ref, kn_ref, vn_ref, kc_ref, vc_ref, gq_ref, gk_ref, _o_prev,
                       o_ref, ko_ref, vo_ref, *, kvh, st, nbuf):
    hd = SWA_HEAD_DIM
    G = SWA_GROUP
    r1 = lax.broadcasted_iota(jnp.int32, (G * SPAD, nbuf), 0) % SPAD
    j1 = lax.broadcasted_iota(jnp.int32, (G * SPAD, nbuf), 1)
    mask_buf = (nbuf + r1 - j1 >= 0) & (nbuf + r1 - j1 <= WINDOW)
    r2 = lax.broadcasted_iota(jnp.int32, (G * SPAD, SPAD), 0) % SPAD
    t2 = lax.broadcasted_iota(jnp.int32, (G * SPAD, SPAD), 1)
    mask_new = (t2 <= r2) & (t2 < st)
    head_of_row = lax.broadcasted_iota(jnp.int32, (G * SPAD, 1), 0) // SPAD
    gq = gq_ref[...]
    gk = gk_ref[...]
    kn = kn_ref[...].astype(F32)
    k_new = []
    outs = []
    for kv in range(kvh):
        sl = slice(kv * hd, (kv + 1) * hd)
        knn = _head_norm(kn[:, sl], gk)
        k_new.append(knn)
        kbuf = kc_ref[:, sl].astype(BF16)
        vbuf = vc_ref[:, sl].astype(BF16)
        knb = knn.astype(BF16)
        vnb = vn_ref[:, sl].astype(BF16)
        h0 = kv * G
        qg = jnp.concatenate(
            [_head_norm(q_ref[:, (h0 + g) * hd:(h0 + g + 1) * hd].astype(F32), gq) for g in range(G)], axis=0)
        qg = (qg * (hd ** -0.5)).astype(BF16)
        sink = jnp.zeros((G * SPAD, 1), F32)
        for g in range(G):
            sink = jnp.where(head_of_row == g, sink_ref[h0 + g], sink)
        s1 = lax.dot_general(qg, kbuf, NT_DIMS, preferred_element_type=F32)
        s2 = lax.dot_general(qg, knb, NT_DIMS, preferred_element_type=F32)
        s1 = jnp.where(mask_buf, s1, NEG_INF)
        s2 = jnp.where(mask_new, s2, NEG_INF)
        og = _softmax_sink_pv([(s1, vbuf), (s2, vnb)], sink)
        outs += [og[g * SPAD:(g + 1) * SPAD, :] for g in range(G)]
    o_ref[...] = jnp.concatenate(outs, axis=1).astype(BF16)
    k_all = jnp.concatenate(k_new, axis=1)
    ko_ref[...] = jnp.concatenate([kc_ref[st:, :], k_all[:st, :]], axis=0)
    vo_ref[...] = jnp.concatenate([vc_ref[st:, :], vn_ref[:st, :].astype(F32)], axis=0)


def _swa_sample(proj_b, o_prev, cache_k, cache_v, sinks_l, gq, gk, l, *, SB, ST, kvh, row_blk0, in_row_blk0):
    M = o_prev.shape[0]
    hd = SWA_HEAD_DIM
    qw = kvh * SWA_GROUP * hd
    kw = kvh * hd
    nbuf = cache_k.shape[2]
    kblk = qw // kw
    grid_spec = pltpu.PrefetchScalarGridSpec(
        num_scalar_prefetch=1,
        grid=(SB,),
        in_specs=[
            pl.BlockSpec((SPAD, qw), lambda b, s: (in_row_blk0 + b, 0)),
            pl.BlockSpec((SPAD, kw), lambda b, s: (in_row_blk0 + b, kblk)),
            pl.BlockSpec((SPAD, kw), lambda b, s: (in_row_blk0 + b, kblk + 1)),
            pl.BlockSpec((None, None, nbuf, kw), lambda b, s: (l, b, 0, 0)),
            pl.BlockSpec((None, None, nbuf, kw), lambda b, s: (l, b, 0, 0)),
            pl.BlockSpec((None, 1, hd), lambda b, s: (l, 0, 0)),
            pl.BlockSpec((None, 1, hd), lambda b, s: (l, 0, 0)),
            pl.BlockSpec(memory_space=pl.ANY),
        ],
        out_specs=[
            pl.BlockSpec((SPAD, qw), lambda b, s: (row_blk0 + b, 0)),
            pl.BlockSpec((None, nbuf, kw), lambda b, s: (b, 0, 0)),
            pl.BlockSpec((None, nbuf, kw), lambda b, s: (b, 0, 0)),
        ],
    )
    return pl.pallas_call(
        functools.partial(_swa_sample_kernel, kvh=kvh, st=ST, nbuf=nbuf),
        grid_spec=grid_spec,
        out_shape=[jax.ShapeDtypeStruct((M, qw), BF16),
                   jax.ShapeDtypeStruct((SB, nbuf, kw), F32),
                   jax.ShapeDtypeStruct((SB, nbuf, kw), F32)],
        input_output_aliases={8: 0},
        compiler_params=_cparams(),
        name="swa_sample",
    )(sinks_l, proj_b, proj_b, proj_b, cache_k, cache_v, gq, gk, o_prev)


def _post_attn_kernel(x_ref, mix_ref, g_ref, gtp_ref, gts_ref, shp_ref, shs_ref, scp_ref, scs_ref,
                      wr_ref, x1_ref, h2_ref, lg_ref, *, npb):
    i = pl.program_id(0)
    gt = _pick_mod(i, npb, gtp_ref, gts_ref)
    x1 = x_ref[...] + gt * mix_ref[...]
    x1_ref[...] = x1
    sh = _pick_mod(i, npb, shp_ref, shs_ref)
    sc = _pick_mod(i, npb, scp_ref, scs_ref)
    h2 = _rms(x1, g_ref[...]) * (1.0 + sc) + sh
    hb = h2.astype(BF16)
    h2_ref[...] = hb
    lg_ref[...] = jnp.dot(hb, wr_ref[...], preferred_element_type=F32)


def _post_attn(rows, x, mix, g, l, modp, mods, wr):
    M, D = x.shape
    gtp, gts = _mod_specs(rows, D, 2)
    shp, shs = _mod_specs(rows, D, 3)
    scp, scs = _mod_specs(rows, D, 4)
    row_spec = pl.BlockSpec((ROW_BLK, D), lambda i: (i, 0))
    wr_spec = pl.BlockSpec((D, LANES), lambda i: (0, 0))
    return pl.pallas_call(
        functools.partial(_post_attn_kernel, npb=rows.npb),
        grid=(rows.nblk,),
        in_specs=[row_spec, row_spec, pl.BlockSpec((None, 1, D), lambda i: (l, 0, 0)),
                  gtp, gts, shp, shs, scp, scs, wr_spec],
        out_specs=[row_spec, row_spec, pl.BlockSpec((ROW_BLK, LANES), lambda i: (i, 0))],
        out_shape=[jax.ShapeDtypeStruct((M, D), F32),
                   jax.ShapeDtypeStruct((M, D), BF16),
                   jax.ShapeDtypeStruct((M, LANES), F32)],
        compiler_params=_cparams(),
        name="post_attn",
    )(x, mix, g, modp, mods, modp, mods, modp, mods, wr)


def _first_block_of_expert(be_ref, i):
    return (i == 0) | (be_ref[i] != be_ref[jnp.maximum(i - 1, 0)])


def _moe_up_kernel(be_ref, na_ref, x_ref, wg_ref, wu_ref, h_ref, wgb, wub):
    i = pl.program_id(1)

    @pl.when(_first_block_of_expert(be_ref, i))
    def _():
        wgb[...] = wg_ref[...].astype(BF16)
        wub[...] = wu_ref[...].astype(BF16)

    @pl.when(i < na_ref[0])
    def _():
        x = x_ref[...]
        a = jnp.dot(x, wgb[...], preferred_element_type=F32)
        u = jnp.dot(x, wub[...], preferred_element_type=F32)
        h_ref[...] = (a * (1.0 / (1.0 + jnp.exp(-a))) * u).astype(BF16)

    @pl.when(i >= na_ref[0])
    def _():
        h_ref[...] = jnp.zeros_like(h_ref)


def _moe_down_kernel(be_ref, na_ref, h_ref, wd_ref, y_ref, wdb):
    i = pl.program_id(1)

    @pl.when(_first_block_of_expert(be_ref, i))
    def _():
        wdb[...] = wd_ref[...].astype(BF16)

    @pl.when(i < na_ref[0])
    def _():
        y_ref[...] = jnp.dot(h_ref[...], wdb[...], preferred_element_type=F32)

    @pl.when(i >= na_ref[0])
    def _():
        y_ref[...] = jnp.zeros_like(y_ref)


def _moe_ffn(xs, block_e, n_active, w_g, w_u, w_d, l):
    P, D = xs.shape
    F = w_g.shape[3]
    nb = P // MOE_BLK
    up_spec = pltpu.PrefetchScalarGridSpec(
        num_scalar_prefetch=2,
        grid=(F // MOE_FC, nb),
        in_specs=[pl.BlockSpec((MOE_BLK, D), lambda j, i, be, na: (i, 0)),
                  pl.BlockSpec((None, None, D, MOE_FC), lambda j, i, be, na: (l, be[i], 0, j)),
                  pl.BlockSpec((None, None, D, MOE_FC), lambda j, i, be, na: (l, be[i], 0, j))],
        out_specs=pl.BlockSpec((MOE_BLK, MOE_FC), lambda j, i, be, na: (i, j)),
        scratch_shapes=[pltpu.VMEM((D, MOE_FC), BF16), pltpu.VMEM((D, MOE_FC), BF16)],
    )
    hmid = pl.pallas_call(
        _moe_up_kernel, grid_spec=up_spec,
        out_shape=jax.ShapeDtypeStruct((P, F), BF16),
        compiler_params=_cparams(), name="moe_up",
    )(block_e, n_active, xs, w_g, w_u)
    down_spec = pltpu.PrefetchScalarGridSpec(
        num_scalar_prefetch=2,
        grid=(D // MOE_DC, nb),
        in_specs=[pl.BlockSpec((MOE_BLK, F), lambda j, i, be, na: (i, 0)),
                  pl.BlockSpec((None, None, F, MOE_DC), lambda j, i, be, na: (l, be[i], 0, j))],
        out_specs=pl.BlockSpec((MOE_BLK, MOE_DC), lambda j, i, be, na: (i, j)),
        scratch_shapes=[pltpu.VMEM((F, MOE_DC), BF16)],
    )
    return pl.pallas_call(
        _moe_down_kernel, grid_spec=down_spec,
        out_shape=jax.ShapeDtypeStruct((P, D), F32),
        compiler_params=_cparams(), name="moe_down",
    )(block_e, n_active, hmid, w_d)


def _route(logits, b_router, valid_row):
    E = b_router.shape[0]
    probs = jax.nn.softmax(logits + b_router.astype(F32), axis=-1)
    pg = probs.reshape(-1, N_GROUPS, E // N_GROUPS)
    group_score = jnp.sum(lax.top_k(pg, TOP_K)[0], axis=-1)
    g_sel = jnp.argmax(group_score, axis=-1)
    in_group = jnp.take_along_axis(pg, g_sel[:, None, None], axis=1)[:, 0]
    vals, idx = lax.top_k(in_group, TOP_K)
    expert_idx = (g_sel[:, None] * (E // N_GROUPS) + idx).astype(jnp.int32)
    gates = vals / jnp.sum(vals, axis=-1, keepdims=True)
    expert_idx = jnp.where(valid_row[:, None], expert_idx, E)
    gates = jnp.where(valid_row[:, None], gates, 0.0)
    return expert_idx, gates


def _dispatch(expert_idx, n_valid, E):
    M = expert_idx.shape[0]
    MK = M * TOP_K
    flat_e = expert_idx.reshape(MK)
    flat_tok = jnp.repeat(jnp.arange(M, dtype=jnp.int32), TOP_K)
    onehot = (flat_e[:, None] == jnp.arange(E, dtype=jnp.int32)[None, :]).astype(jnp.int32)
    oh3 = onehot.reshape(MK // LANES, LANES, E)
    within = jnp.cumsum(oh3, axis=1)
    blk_tot = within[:, -1, :]
    blk_off = jnp.cumsum(blk_tot, axis=0) - blk_tot
    rank = jnp.sum((within + blk_off[:, None, :] - 1) * oh3, axis=-1).reshape(MK)
    counts = jnp.sum(blk_tot, axis=0)
    padded = (counts + MOE_BLK - 1) // MOE_BLK * MOE_BLK
    pad_end = jnp.cumsum(padded)
    pad_start = pad_end - padded
    nb = (n_valid * TOP_K + E * (MOE_BLK - 1)) // MOE_BLK + 1
    P = nb * MOE_BLK
    seg_start = jnp.sum(onehot * pad_start[None, :], axis=-1)
    dest = jnp.where(flat_e < E, seg_start + rank, P)
    slot_tok = jnp.full((P,), M, jnp.int32).at[dest].set(flat_tok, mode="drop")
    blk_row0 = jnp.arange(nb, dtype=jnp.int32) * MOE_BLK
    block_e = jnp.minimum(jnp.sum(blk_row0[:, None] >= pad_end[None, :], axis=-1), E - 1).astype(jnp.int32)
    n_active = (pad_end[-1:] // MOE_BLK).astype(jnp.int32)
    pair_dest = jnp.minimum(dest, P - 1).reshape(M, TOP_K)
    return slot_tok, block_e, n_active, pair_dest


def _post_moe_kernel(x_ref, y0_ref, y1_ref, gw_ref, gtp_ref, gts_ref, *rest, npb, with_norm):
    i = pl.program_id(0)
    gt = _pick_mod(i, npb, gtp_ref, gts_ref)
    gw = gw_ref[...]
    y = gw[:, 0:1] * y0_ref[...].astype(F32) + gw[:, 1:2] * y1_ref[...].astype(F32)
    x2 = x_ref[...] + gt * y
    if with_norm:
        g_ref, shp_ref, shs_ref, scp_ref, scs_ref, x2_ref, h_ref = rest
        x2_ref[...] = x2
        sh = _pick_mod(i, npb, shp_ref, shs_ref)
        sc = _pick_mod(i, npb, scp_ref, scs_ref)
        h_ref[...] = (_rms(x2, g_ref[...]) * (1.0 + sc) + sh).astype(BF16)
    else:
        (x2_ref,) = rest
        x2_ref[...] = x2


def _post_moe(rows, x1, y0, y1, gates_pad, l, modp, mods, g_next=None, modp_next=None, mods_next=None):
    M, D = x1.shape
    with_norm = g_next is not None
    gtp, gts = _mod_specs(rows, D, 5)
    row_spec = pl.BlockSpec((ROW_BLK, D), lambda i: (i, 0))
    in_specs = [row_spec, row_spec, row_spec, pl.BlockSpec((ROW_BLK, LANES), lambda i: (i, 0)), gtp, gts]
    args = [x1, y0, y1, gates_pad, modp, mods]
    out_specs = [row_spec]
    out_shape = [jax.ShapeDtypeStruct((M, D), F32)]
    if with_norm:
        shp, shs = _mod_specs(rows, D, 0)
        scp, scs = _mod_specs(rows, D, 1)
        in_specs += [pl.BlockSpec((None, 1, D), lambda i: (l + 1, 0, 0)), shp, shs, scp, scs]
        args += [g_next, modp_next, mods_next, modp_next, mods_next]
        out_specs.append(row_spec)
        out_shape.append(jax.ShapeDtypeStruct((M, D), BF16))
    return pl.pallas_call(
        functools.partial(_post_moe_kernel, npb=rows.npb, with_norm=with_norm),
        grid=(rows.nblk,),
        in_specs=in_specs, out_specs=out_specs, out_shape=out_shape,
        compiler_params=_cparams(),
        name="post_moe",
    )(*args)


def kernel(x_prompt, x_sample, c_prompt, c_sample, state_gla, cache_swa_k, cache_swa_v, w_ada, b_ada, g_norm1, g_norm2, w_in, w_gate_up, b_gate, g_gla_out, g_q_norm, g_k_norm, sinks, w_out, w_router, b_router, w_e_gate, w_e_up, w_e_down):
    B, T, D = x_prompt.shape
    SB, ST, _ = x_sample.shape
    L = w_ada.shape[0]
    H = GLA_HEADS
    dk = state_gla.shape[3]
    dv = state_gla.shape[4]
    nbuf, kvh, hd = cache_swa_k.shape[2:]
    E = w_router.shape[1]
    rows = _Rows(B, T, SB, ST)
    M = rows.M

    gla_cols = 2 * H * dk + 2 * H * dv
    swa_col0 = gla_cols + GATE_RANK
    swa_cols = w_in.shape[2] - swa_col0

    xs_pad = jnp.pad(x_sample, ((0, 0), (0, SPAD - ST), (0, 0))).reshape(SB * SPAD, D)
    x = jnp.concatenate([x_prompt.reshape(B * T, D), xs_pad], axis=0)
    row_id = jnp.arange(M, dtype=jnp.int32)
    valid_row = (row_id < rows.n_prompt) | ((row_id - rows.n_prompt) % SPAD < ST)
    n_valid = rows.n_prompt + SB * ST

    mod = _ada_mod(jnp.concatenate([c_prompt, c_sample], axis=0), w_ada, b_ada)
    modp_all = mod[:, :B].reshape(L, B, 1, 6 * D)
    mods_all = jnp.repeat(mod[:, B:], SPAD, axis=1)

    wgu_pad = jnp.pad(w_gate_up, ((0, 0), (0, LANES - GATE_RANK), (0, 0))).astype(BF16)
    bg3 = b_gate.reshape(L, 1, H * dk)
    go3 = g_gla_out.reshape(L, 1, dv)
    gq3 = g_q_norm.reshape(L, 1, hd)
    gk3 = g_k_norm.reshape(L, 1, hd)
    g1 = g_norm1.reshape(L, 1, D)
    g2 = g_norm2.reshape(L, 1, D)
    wr = jnp.pad(w_router, ((0, 0), (0, LANES - E))).astype(BF16)
    tm = MM_ROWS
    tail_blk0 = (rows.n_prompt - (M - tm)) // SPAD
    assert M % tm == 0 and M - tm <= rows.n_prompt and (rows.n_prompt - (M - tm)) % SPAD == 0
    ck =cache_swa_k.reshape(L, SB, nbuf, kvh * hd)
    cv = cache_swa_v.reshape(L, SB, nbuf, kvh * hd)

    h = _norm_mod(rows, x, g1, 0, modp_all[0], mods_all[0])
    kp_l, vp_l, ks_l, vs_l = [], [], [], []
    s_p = s_s = None
    for l in range(L):
        modp, mods = modp_all[l], mods_all[l]
        proj_a, proj_a_s = _matmul([h], w_in, l, 0, gla_cols, 512, tm, BF16, "proj_gla", with_tail=True)
        (glr,) = _matmul([h], w_in, l, gla_cols // LANES, LANES, LANES, tm, BF16, "proj_gate_lr")
        w_swa = w_in[l, :, swa_col0:].astype(BF16)
        proj_b, proj_b_s = _matmul([h], w_swa, None, 0, swa_cols, 512, tm, BF16, "proj_swa", with_tail=True)

        o_gla, s_p = _gla(proj_a, glr, wgu_pad, bg3, go3, l, M=M, L=L, n_seq=B, n_chunks=T // GLA_CHUNK,
                          C=GLA_CHUNK, valid=GLA_CHUNK, row_blk0=0, stack=s_p)
        o_gla, s_s = _gla(proj_a_s, glr, wgu_pad, bg3, go3, l, M=M, L=L, n_seq=SB, n_chunks=1, C=SPAD, valid=ST,
                          row_blk0=rows.n_prompt // SPAD, in_row_blk0=tail_blk0, s0=state_gla, o_prev=o_gla,
                          stack=s_s)
        o_swa, k_p, v_p = _swa_prompt(proj_b, sinks[l], gq3, gk3, l, B=B, T=T, kvh=kvh)
        o_swa, k_s, v_s = _swa_sample(proj_b_s, o_swa, ck, cv, sinks[l], gq3, gk3, l, SB=SB, ST=ST, kvh=kvh,
                                      row_blk0=rows.n_prompt // SPAD, in_row_blk0=tail_blk0)

        (mix,) = _matmul([o_gla, o_swa], w_out, l, 0, D, 512, tm, F32, "out_proj")
        x1, h2, logits = _post_attn(rows, x, mix, g2, l, modp, mods, wr)

        expert_idx, gates = _route(logits[:, :E], b_router, valid_row)
        slot_tok, block_e, n_active, pair_dest = _dispatch(expert_idx, n_valid, E)
        h2_ext = jnp.concatenate([h2, jnp.zeros((1, D), BF16)], axis=0)
        ys = _moe_ffn(h2_ext[slot_tok], block_e, n_active, w_e_gate, w_e_up, w_e_down, l)
        y0 = ys[pair_dest[:, 0]]
        y1 = ys[pair_dest[:, 1]]
        gates_pad = jnp.pad(gates, ((0, 0), (0, LANES - TOP_K)))
        if l + 1 < L:
            x, h = _post_moe(rows, x1, y0, y1, gates_pad, l, modp, mods, g1, modp_all[l + 1], mods_all[l + 1])
        else:
            (x,) = _post_moe(rows, x1, y0, y1, gates_pad, l, modp, mods)

        kp_l.append(k_p.reshape(B, WINDOW, kvh, hd))
        vp_l.append(v_p.reshape(B, WINDOW, kvh, hd))
        ks_l.append(k_s.reshape(SB, nbuf, kvh, hd))
        vs_l.append(v_s.reshape(SB, nbuf, kvh, hd))

    y_prompt = x[:rows.n_prompt].reshape(B, T, D)
    y_sample = x[rows.n_prompt:].reshape(SB, SPAD, D)[:, :ST]
    return (y_prompt, y_sample, s_p, jnp.stack(kp_l), jnp.stack(vp_l),
            s_s, jnp.stack(ks_l), jnp.stack(vs_l))
```

```python
import functools

import jax
import jax.numpy as jnp
from jax import lax
from jax.experimental import pallas as pl
from jax.experimental.pallas import tpu as pltpu

F32 = jnp.float32
BF16 = jnp.bfloat16

GLA_HEADS = 4
GATE_RANK = 16
GATE_TEMP = 16.0
SWA_HEAD_DIM = 64
SWA_GROUP = 8
WINDOW = 128
N_GROUPS = 4
TOP_K = 2
EPS = 1e-6
NEG_INF = -1e30

LANES = 128
BF16_SUBLANES = 16
VMEM_LIMIT = 48 * 1024 * 1024

SPAD = BF16_SUBLANES
ROW_BLK = 128
GLA_CHUNK = 128
MM_ROWS = 1088
MOE_BLK = 256
MOE_FC = 512
MOE_DC = 2048


def _cparams(**kw):
    return pltpu.CompilerParams(vmem_limit_bytes=VMEM_LIMIT, **kw)


def _ada_kernel(c_ref, w_ref, b_ref, o_ref):
    c = c_ref[...]
    a = (c * (1.0 / (1.0 + jnp.exp(-c)))).astype(BF16)
    o_ref[...] = jnp.dot(a, w_ref[...].astype(BF16), preferred_element_type=F32) + b_ref[...]


def _ada_mod(c_all, w_ada, b_ada):
    L, D, N = w_ada.shape
    R = c_all.shape[0]
    tn = 512
    return pl.pallas_call(
        _ada_kernel,
        grid=(L, N // tn),
        in_specs=[pl.BlockSpec((R, D), lambda l, j: (0, 0)),
                  pl.BlockSpec((None, D, tn), lambda l, j: (l, 0, j)),
                  pl.BlockSpec((None, 1, tn), lambda l, j: (l, 0, j))],
        out_specs=pl.BlockSpec((None, R, tn), lambda l, j: (l, 0, j)),
        out_shape=jax.ShapeDtypeStruct((L, R, N), F32),
        compiler_params=_cparams(),
        name="ada_mod",
    )(c_all, w_ada, b_ada.reshape(L, 1, N))


class _Rows:
    def __init__(self, B, T, SB, ST):
        self.B, self.T, self.SB, self.ST = B, T, SB, ST
        self.n_prompt = B * T
        self.n_sample = SB * SPAD
        self.M = self.n_prompt + self.n_sample
        assert T % ROW_BLK == 0 and self.n_sample % ROW_BLK == 0 and ST <= SPAD
        self.npb = self.n_prompt // ROW_BLK
        self.nsb = self.n_sample // ROW_BLK
        self.nblk = self.npb + self.nsb
        self.blk_per_seq = T // ROW_BLK


def _mod_specs(rows, D, col):
    npb, bps, B = rows.npb, rows.blk_per_seq, rows.B
    p_spec = pl.BlockSpec((None, 1, D), lambda i: (jnp.minimum(i // bps, B - 1), 0, col))
    s_spec = pl.BlockSpec((ROW_BLK // SPAD, D), lambda i: (jnp.maximum(i - npb, 0), col))
    return p_spec, s_spec


def _pick_mod(i, npb, p_ref, s_ref):
    s = s_ref[...]
    s_rows = jnp.concatenate(
        [jnp.broadcast_to(s[r:r + 1, :], (SPAD, s.shape[1])) for r in range(s.shape[0])], axis=0)
    return jnp.where(i < npb, p_ref[...], s_rows)


def _rms(x, g):
    return x * lax.rsqrt(jnp.mean(x * x, axis=-1, keepdims=True) + EPS) * g


def _norm_mod_kernel(x_ref, g_ref, shp_ref, shs_ref, scp_ref, scs_ref, h_ref, *, npb):
    i = pl.program_id(0)
    y = _rms(x_ref[...], g_ref[...])
    sh = _pick_mod(i, npb, shp_ref, shs_ref)
    sc = _pick_mod(i, npb, scp_ref, scs_ref)
    h_ref[...] = (y * (1.0 + sc) + sh).astype(BF16)


def _norm_mod(rows, x, g, l, modp, mods):
    M, D = x.shape
    shp, shs = _mod_specs(rows, D, 0)
    scp, scs = _mod_specs(rows, D, 1)
    row_spec = pl.BlockSpec((ROW_BLK, D), lambda i: (i, 0))
    return pl.pallas_call(
        functools.partial(_norm_mod_kernel, npb=rows.npb),
        grid=(rows.nblk,),
        in_specs=[row_spec, pl.BlockSpec((None, 1, D), lambda i: (l, 0, 0)), shp, shs, scp, scs],
        out_specs=row_spec,
        out_shape=jax.ShapeDtypeStruct((M, D), BF16),
        compiler_params=_cparams(),
        name="norm_mod",
    )(x, g, modp, mods, modp, mods)


def _mm_kernel(*refs, n_lhs, with_tail):
    x_refs, w_ref, o_ref = refs[:n_lhs], refs[n_lhs], refs[n_lhs + 1]
    wbf_ref = refs[-1]

    @pl.when(pl.program_id(1) == 0)
    def _():
        wbf_ref[...] = w_ref[...].astype(BF16)

    acc = None
    k0 = 0
    for x_ref in x_refs:
        kk = x_ref.shape[1]
        part = jnp.dot(x_ref[...], wbf_ref[k0:k0 + kk, :], preferred_element_type=F32)
        acc = part if acc is None else acc + part
        k0 += kk
    o_ref[...] = acc.astype(o_ref.dtype)
    if with_tail:
        @pl.when(pl.program_id(1) == pl.num_programs(1) - 1)
        def _():
            refs[n_lhs + 2][...] = acc


def _matmul(xs, w, w_lead, col_blk0, n_cols, tn, tm, out_dtype, name, with_tail=False):
    M = xs[0].shape[0]
    K = sum(x.shape[1] for x in xs)
    if w.ndim == 3:
        w_spec = pl.BlockSpec((None, K, tn), lambda j, i: (w_lead, 0, col_blk0 + j))
    else:
        w_spec = pl.BlockSpec((K, tn), lambda j, i: (0, col_blk0 + j))
    x_specs = [pl.BlockSpec((tm, x.shape[1]), lambda j, i: (i, 0)) for x in xs]
    out_specs = [pl.BlockSpec((tm, tn), lambda j, i: (i, j))]
    out_shape = [jax.ShapeDtypeStruct((M, n_cols), out_dtype)]
    if with_tail:
        out_specs.append(pl.BlockSpec((tm, tn), lambda j, i: (0, j)))
        out_shape.append(jax.ShapeDtypeStruct((tm, n_cols), F32))
    return pl.pallas_call(
        functools.partial(_mm_kernel, n_lhs=len(xs), with_tail=with_tail),
        grid=(n_cols // tn, M // tm),
        in_specs=x_specs + [w_spec],
        out_specs=out_specs,
        out_shape=out_shape,
        scratch_shapes=[pltpu.VMEM((K, tn), BF16)],
        compiler_params=_cparams(),
        name=name,
    )(*xs, w)


def _gla_kernel(*refs, C, valid, has_s0, dk, dv):
    q_ref, k_ref, v_ref, gg_ref, glr_ref, wgu_ref, bg_ref, go_ref = refs[:8]
    s0_ref = refs[8] if has_s0 else None
    o_ref, sf_ref, S = refs[-3:]
    c = pl.program_id(2)

    @pl.when(c == 0)
    def _():
        if has_s0:
            S[...] = s0_ref[...]
        else:
            S[...] = jnp.zeros_like(S)

    z = jnp.dot(glr_ref[...], wgu_ref[...], preferred_element_type=F32) + bg_ref[...]
    la = (jnp.minimum(z, 0.0) - jnp.log(1.0 + jnp.exp(-jnp.abs(z)))) * (1.0 / GATE_TEMP)
    q = q_ref[...].astype(F32) * (dk ** -0.5)
    k = k_ref[...].astype(F32)
    if valid < C:
        live = lax.broadcasted_iota(jnp.int32, (C, 1), 0) < valid
        la = jnp.where(live, la, 0.0)
        k = jnp.where(live, k, 0.0)

    la_hi = la.astype(BF16).astype(F32)
    la_lo = (la - la_hi).astype(BF16).astype(F32)
    ri = lax.broadcasted_iota(jnp.int32, (C, C), 0)
    ci = lax.broadcasted_iota(jnp.int32, (C, C), 1)
    tril = ri >= ci
    trilf = tril.astype(F32)
    b = (jnp.dot(trilf, la_hi, preferred_element_type=F32)
         + jnp.dot(trilf, la_lo, preferred_element_type=F32))

    q_dec = (q * jnp.exp(b)).astype(BF16)
    if C > SPAD:
        r = b[C // 2:C // 2 + 1, :]
        q_in = (q * jnp.exp(b - r)).astype(BF16)
        k_in = (k * jnp.exp(r - b)).astype(BF16)
    else:
        q_in = q_dec
        k_in = (k * jnp.exp(-b)).astype(BF16)
    att = lax.dot_general(q_in, k_in, (((1,), (1,)), ((), ())), preferred_element_type=F32)
    att = jnp.where(tril, att, 0.0)

    s_old = S[...]
    v = v_ref[...].astype(BF16)
    o = (jnp.dot(att.astype(BF16), v, preferred_element_type=F32)
         + jnp.dot(q_dec, s_old.astype(BF16), preferred_element_type=F32))

    b_end = b[valid - 1:valid, :]
    k_end = (k * jnp.exp(b_end - b)).astype(BF16)
    tn_dims = (((0,), (0,)), ((), ()))
    upd = lax.dot_general(k_end, v, tn_dims, preferred_element_type=F32)
    ones = jnp.ones((C, LANES), F32)
    b_col = (lax.dot_general(la_hi, ones, tn_dims, preferred_element_type=F32)
             + lax.dot_general(la_lo, ones, tn_dims, preferred_element_type=F32))
    dec = jnp.exp(b_col)
    S[...] = jnp.concatenate([dec] * (dv // LANES), axis=1) * s_old + upd

    gg = gg_ref[...].astype(F32)
    on = _rms(o, go_ref[...])
    o_ref[...] = (on * (gg * (1.0 / (1.0 + jnp.exp(-gg))))).astype(BF16)

    @pl.when(c == pl.num_programs(2) - 1)
    def _():
        sf_ref[...] = S[...]


def _gla(proj_a, glr, wgu, bg, go, l, *, M, L, n_seq, n_chunks, C, valid, row_blk0, in_row_blk0=None,
         s0=None, o_prev=None, stack=None):
    H = GLA_HEADS
    dk = wgu.shape[2] // H
    dv = go.shape[2]
    has_s0 = s0 is not None
    if in_row_blk0 is None:
        in_row_blk0 = row_blk0

    def rb(b, c):
        return row_blk0 + b * n_chunks + c

    def rbi(b, c):
        return in_row_blk0 + b * n_chunks + c

    in_specs = [
        pl.BlockSpec((C, dk), lambda b, h, c: (rbi(b, c), h)),
        pl.BlockSpec((C, dk), lambda b, h, c: (rbi(b, c), H + h)),
        pl.BlockSpec((C, dv), lambda b, h, c: (rbi(b, c), (2 * H * dk) // dv + h)),
        pl.BlockSpec((C, dv), lambda b, h, c: (rbi(b, c), (2 * H * dk) // dv + H + h)),
        pl.BlockSpec((C, LANES), lambda b, h, c: (rb(b, c), 0)),
        pl.BlockSpec((None, LANES, dk), lambda b, h, c: (l, 0, h)),
        pl.BlockSpec((None, 1, dk), lambda b, h, c: (l, 0, h)),
        pl.BlockSpec((None, 1, dv), lambda b, h, c: (l, 0, 0)),
    ]
    args = [proj_a, proj_a, proj_a, proj_a, glr, wgu, bg, go]
    aliases = {}
    if has_s0:
        in_specs += [pl.BlockSpec((None, None, None, dk, dv), lambda b, h, c: (l, b, h, 0, 0)),
                     pl.BlockSpec(memory_space=pl.ANY)]
        args += [s0, o_prev]
        aliases = {len(args) - 1: 0}
    if stack is not None:
        in_specs.append(pl.BlockSpec(memory_space=pl.ANY))
        args.append(stack)
        aliases[len(args) - 1] = 1
    return pl.pallas_call(
        functools.partial(_gla_kernel, C=C, valid=valid, has_s0=has_s0, dk=dk, dv=dv),
        grid=(n_seq, H, n_chunks),
        in_specs=in_specs,
        out_specs=[pl.BlockSpec((C, dv), lambda b, h, c: (rb(b, c), h)),
                   pl.BlockSpec((None, None, None, dk, dv), lambda b, h, c: (l, b, h, 0, 0))],
        out_shape=[jax.ShapeDtypeStruct((M, H * dv), BF16),
                   jax.ShapeDtypeStruct((L, n_seq, H, dk, dv), F32)],
        scratch_shapes=[pltpu.VMEM((dk, dv), F32)],
        input_output_aliases=aliases,
        compiler_params=_cparams(),
        name="gla_sample" if has_s0 else "gla_prompt",
    )(*args)


def _head_norm(x, g):
    return x * lax.rsqrt(jnp.mean(x * x, axis=-1, keepdims=True) + EPS) * g


def _segment_rms_inv(x, seg):
    R, Wd = x.shape
    assert Wd // seg <= LANES
    sel = (lax.broadcasted_iota(jnp.int32, (Wd, LANES), 0) // seg
           == lax.broadcasted_iota(jnp.int32, (Wd, LANES), 1)).astype(F32).astype(BF16)
    sel_t = (lax.broadcasted_iota(jnp.int32, (LANES, Wd), 0)
             == lax.broadcasted_iota(jnp.int32, (LANES, Wd), 1) // seg).astype(F32).astype(BF16)

    def split(a):
        hi = a.astype(BF16)
        return hi, (a - hi.astype(F32)).astype(BF16)

    sq_hi, sq_lo = split(x * x)
    sums = (jnp.dot(sq_hi, sel, preferred_element_type=F32)
            + jnp.dot(sq_lo, sel, preferred_element_type=F32))
    inv_hi, inv_lo = split(lax.rsqrt(sums * (1.0 / seg) + EPS))
    return (jnp.dot(inv_hi, sel_t, preferred_element_type=F32)
            + jnp.dot(inv_lo, sel_t, preferred_element_type=F32))


def _softmax_sink_pv(parts, sink):
    m = sink
    for s, _ in parts:
        m = jnp.maximum(m, jnp.max(s, axis=-1, keepdims=True))
    den = jnp.exp(sink - m)
    es = []
    for s, _ in parts:
        e = jnp.exp(s - m)
        den = den + jnp.sum(e, axis=-1, keepdims=True)
        es.append(e)
    inv = 1.0 / den
    acc = None
    for e, (_, v) in zip(es, parts):
        pv = jnp.dot((e * inv).astype(BF16), v, preferred_element_type=F32)
        acc = pv if acc is None else acc + pv
    return acc


NT_DIMS = (((1,), (1,)), ((), ()))


def _swa_prompt_kernel(sink_ref, q_ref, kp_ref, kc_ref, vp_ref, vc_ref, gq_ref, gk_ref,
                       o_ref, ko_ref, vo_ref, *, kvh):
    i = pl.program_id(1)
    hd = SWA_HEAD_DIM
    W = WINDOW
    G = SWA_GROUP
    r = lax.broadcasted_iota(jnp.int32, (G * W, 2 * W), 0) % W
    j = lax.broadcasted_iota(jnp.int32, (G * W, 2 * W), 1)
    mask = (j >= r) & (j <= r + W) & ((j >= W) | (i > 0))
    head_of_row = lax.broadcasted_iota(jnp.int32, (G * W, 1), 0) // W
    gk = gk_ref[...]
    q = q_ref[...].astype(F32)
    gq_row = jnp.concatenate([gq_ref[...]] * (q.shape[1] // hd), axis=1)
    qn = (q * _segment_rms_inv(q, hd) * gq_row * (hd ** -0.5)).astype(BF16)
    kp = kp_ref[...].astype(F32)
    kc = kc_ref[...].astype(F32)
    k_new = []
    outs = []
    for kv in range(kvh):
        sl = slice(kv * hd, (kv + 1) * hd)
        kcn = _head_norm(kc[:, sl], gk)
        k_new.append(kcn)
        kcat = jnp.concatenate([_head_norm(kp[:, sl], gk), kcn], axis=0).astype(BF16)
        vcat = jnp.concatenate([vp_ref[:, sl], vc_ref[:, sl]], axis=0)
        h0 = kv * G
        qg = jnp.concatenate([qn[:, (h0 + g) * hd:(h0 + g + 1) * hd] for g in range(G)], axis=0)
        sink = jnp.zeros((G * W, 1), F32)
        for g in range(G):
            sink = jnp.where(head_of_row == g, sink_ref[h0 + g], sink)
        s = lax.dot_general(qg, kcat, NT_DIMS, preferred_element_type=F32)
        s = jnp.where(mask, s, NEG_INF)
        og = _softmax_sink_pv([(s, vcat)], sink)
        outs += [og[g * W:(g + 1) * W, :] for g in range(G)]
    o_ref[...] = jnp.concatenate(outs, axis=1).astype(BF16)
    ko_ref[...] = jnp.concatenate(k_new, axis=1)
    vo_ref[...] = vc_ref[...].astype(F32)


def _swa_prompt(proj_b, sinks_l, gq, gk, l, *, B, T, kvh):
    M = proj_b.shape[0]
    hd = SWA_HEAD_DIM
    qw = kvh * SWA_GROUP * hd
    kw = kvh * hd
    nb = T // WINDOW
    kblk = qw // kw

    def cur(b, i, s):
        return b * nb + i

    def prev(b, i, s):
        return b * nb + jnp.maximum(i - 1, 0)

    grid_spec = pltpu.PrefetchScalarGridSpec(
        num_scalar_prefetch=1,
        grid=(B, nb),
        in_specs=[
            pl.BlockSpec((WINDOW, qw), lambda b, i, s: (cur(b, i, s), 0)),
            pl.BlockSpec((WINDOW, kw), lambda b, i, s: (prev(b, i, s), kblk)),
            pl.BlockSpec((WINDOW, kw), lambda b, i, s: (cur(b, i, s), kblk)),
            pl.BlockSpec((WINDOW, kw), lambda b, i, s: (prev(b, i, s), kblk + 1)),
            pl.BlockSpec((WINDOW, kw), lambda b, i, s: (cur(b, i, s), kblk + 1)),
            pl.BlockSpec((None, 1, hd), lambda b, i, s: (l, 0, 0)),
            pl.BlockSpec((None, 1, hd), lambda b, i, s: (l, 0, 0)),
        ],
        out_specs=[
            pl.BlockSpec((WINDOW, qw), lambda b, i, s: (cur(b, i, s), 0)),
            pl.BlockSpec((None, WINDOW, kw), lambda b, i, s: (b, 0, 0)),
            pl.BlockSpec((None, WINDOW, kw), lambda b, i, s: (b, 0, 0)),
        ],
    )
    return pl.pallas_call(
        functools.partial(_swa_prompt_kernel, kvh=kvh),
        grid_spec=grid_spec,
        out_shape=[jax.ShapeDtypeStruct((M, qw), BF16),
                   jax.ShapeDtypeStruct((B, WINDOW, kw), F32),
                   jax.ShapeDtypeStruct((B, WINDOW, kw), F32)],
        compiler_params=_cparams(),
        name="swa_prompt",
    )(sinks_l, proj_b, proj_b, proj_b, proj_b, proj_b, gq, gk)


def _swa_sample_kernel(sink_ref, q_ref, kn_ref, vn_ref, kc_ref, vc_ref, gq_ref, gk_ref, _o_prev,
                       o_ref, ko_ref, vo_ref, *, kvh, st, nbuf):
    hd = SWA_HEAD_DIM
    G = SWA_GROUP
    r1 = lax.broadcasted_iota(jnp.int32, (G * SPAD, nbuf), 0) % SPAD
    j1 = lax.broadcasted_iota(jnp.int32, (G * SPAD, nbuf), 1)
    mask_buf = (nbuf + r1 - j1 >= 0) & (nbuf + r1 - j1 <= WINDOW)
    r2 = lax.broadcasted_iota(jnp.int32, (G * SPAD, SPAD), 0) % SPAD
    t2 = lax.broadcasted_iota(jnp.int32, (G * SPAD, SPAD), 1)
    mask_new = (t2 <= r2) & (t2 < st)
    head_of_row = lax.broadcasted_iota(jnp.int32, (G * SPAD, 1), 0) // SPAD
    gq = gq_ref[...]
    gk = gk_ref[...]
    kn = kn_ref[...].astype(F32)
    k_new = []
    outs = []
    for kv in range(kvh):
        sl = slice(kv * hd, (kv + 1) * hd)
        knn = _head_norm(kn[:, sl], gk)
        k_new.append(knn)
        kbuf = kc_ref[:, sl].astype(BF16)
        vbuf = vc_ref[:, sl].astype(BF16)
        knb = knn.astype(BF16)
        vnb = vn_ref[:, sl].astype(BF16)
        h0 = kv * G
        qg = jnp.concatenate(
            [_head_norm(q_ref[:, (h0 + g) * hd:(h0 + g + 1) * hd].astype(F32), gq) for g in range(G)], axis=0)
        qg = (qg * (hd ** -0.5)).astype(BF16)
        sink = jnp.zeros((G * SPAD, 1), F32)
        for g in range(G):
            sink = jnp.where(head_of_row == g, sink_ref[h0 + g], sink)
        s1 = lax.dot_general(qg, kbuf, NT_DIMS, preferred_element_type=F32)
        s2 = lax.dot_general(qg, knb, NT_DIMS, preferred_element_type=F32)
        s1 = jnp.where(mask_buf, s1, NEG_INF)
        s2 = jnp.where(mask_new, s2, NEG_INF)
        og = _softmax_sink_pv([(s1, vbuf), (s2, vnb)], sink)
        outs += [og[g * SPAD:(g + 1) * SPAD, :] for g in range(G)]
    o_ref[...] = jnp.concatenate(outs, axis=1).astype(BF16)
    k_all = jnp.concatenate(k_new, axis=1)
    ko_ref[...] = jnp.concatenate([kc_ref[st:, :], k_all[:st, :]], axis=0)
    vo_ref[...] = jnp.concatenate([vc_ref[st:, :], vn_ref[:st, :].astype(F32)], axis=0)


def _swa_sample(proj_b, o_prev, cache_k, cache_v, sinks_l, gq, gk, l, *, SB, ST, kvh, row_blk0, in_row_blk0):
    M = o_prev.shape[0]
    hd = SWA_HEAD_DIM
    qw = kvh * SWA_GROUP * hd
    kw = kvh * hd
    nbuf = cache_k.shape[2]
    kblk = qw // kw
    grid_spec = pltpu.PrefetchScalarGridSpec(
        num_scalar_prefetch=1,
        grid=(SB,),
        in_specs=[
            pl.BlockSpec((SPAD, qw), lambda b, s: (in_row_blk0 + b, 0)),
            pl.BlockSpec((SPAD, kw), lambda b, s: (in_row_blk0 + b, kblk)),
            pl.BlockSpec((SPAD, kw), lambda b, s: (in_row_blk0 + b, kblk + 1)),
            pl.BlockSpec((None, None, nbuf, kw), lambda b, s: (l, b, 0, 0)),
            pl.BlockSpec((None, None, nbuf, kw), lambda b, s: (l, b, 0, 0)),
            pl.BlockSpec((None, 1, hd), lambda b, s: (l, 0, 0)),
            pl.BlockSpec((None, 1, hd), lambda b, s: (l, 0, 0)),
            pl.BlockSpec(memory_space=pl.ANY),
        ],
        out_specs=[
            pl.BlockSpec((SPAD, qw), lambda b, s: (row_blk0 + b, 0)),
            pl.BlockSpec((None, nbuf, kw), lambda b, s: (b, 0, 0)),
            pl.BlockSpec((None, nbuf, kw), lambda b, s: (b, 0, 0)),
        ],
    )
    return pl.pallas_call(
        functools.partial(_swa_sample_kernel, kvh=kvh, st=ST, nbuf=nbuf),
        grid_spec=grid_spec,
        out_shape=[jax.ShapeDtypeStruct((M, qw), BF16),
                   jax.ShapeDtypeStruct((SB, nbuf, kw), F32),
                   jax.ShapeDtypeStruct((SB, nbuf, kw), F32)],
        input_output_aliases={8: 0},
        compiler_params=_cparams(),
        name="swa_sample",
    )(sinks_l, proj_b, proj_b, proj_b, cache_k, cache_v, gq, gk, o_prev)


def _post_attn_kernel(x_ref, mix_ref, g_ref, gtp_ref, gts_ref, shp_ref, shs_ref, scp_ref, scs_ref,
                      wr_ref, x1_ref, h2_ref, lg_ref, *, npb):
    i = pl.program_id(0)
    gt = _pick_mod(i, npb, gtp_ref, gts_ref)
    x1 = x_ref[...] + gt * mix_ref[...]
    x1_ref[...] = x1
    sh = _pick_mod(i, npb, shp_ref, shs_ref)
    sc = _pick_mod(i, npb, scp_ref, scs_ref)
    h2 = _rms(x1, g_ref[...]) * (1.0 + sc) + sh
    hb = h2.astype(BF16)
    h2_ref[...] = hb
    lg_ref[...] = jnp.dot(hb, wr_ref[...], preferred_element_type=F32)


def _post_attn(rows, x, mix, g, l, modp, mods, wr):
    M, D = x.shape
    gtp, gts = _mod_specs(rows, D, 2)
    shp, shs = _mod_specs(rows, D, 3)
    scp, scs = _mod_specs(rows, D, 4)
    row_spec = pl.BlockSpec((ROW_BLK, D), lambda i: (i, 0))
    wr_spec = pl.BlockSpec((D, LANES), lambda i: (0, 0))
    return pl.pallas_call(
        functools.partial(_post_attn_kernel, npb=rows.npb),
        grid=(rows.nblk,),
        in_specs=[row_spec, row_spec, pl.BlockSpec((None, 1, D), lambda i: (l, 0, 0)),
                  gtp, gts, shp, shs, scp, scs, wr_spec],
        out_specs=[row_spec, row_spec, pl.BlockSpec((ROW_BLK, LANES), lambda i: (i, 0))],
        out_shape=[jax.ShapeDtypeStruct((M, D), F32),
                   jax.ShapeDtypeStruct((M, D), BF16),
                   jax.ShapeDtypeStruct((M, LANES), F32)],
        compiler_params=_cparams(),
        name="post_attn",
    )(x, mix, g, modp, mods, modp, mods, modp, mods, wr)


def _first_block_of_expert(be_ref, i):
    return (i == 0) | (be_ref[i] != be_ref[jnp.maximum(i - 1, 0)])


def _moe_up_kernel(be_ref, na_ref, x_ref, wg_ref, wu_ref, h_ref, wgb, wub):
    i = pl.program_id(1)

    @pl.when(_first_block_of_expert(be_ref, i))
    def _():
        wgb[...] = wg_ref[...].astype(BF16)
        wub[...] = wu_ref[...].astype(BF16)

    @pl.when(i < na_ref[0])
    def _():
        x = x_ref[...]
        a = jnp.dot(x, wgb[...], preferred_element_type=F32)
        u = jnp.dot(x, wub[...], preferred_element_type=F32)
        h_ref[...] = (a * (1.0 / (1.0 + jnp.exp(-a))) * u).astype(BF16)

    @pl.when(i >= na_ref[0])
    def _():
        h_ref[...] = jnp.zeros_like(h_ref)


def _moe_down_kernel(be_ref, na_ref, h_ref, wd_ref, y_ref, wdb):
    i = pl.program_id(1)

    @pl.when(_first_block_of_expert(be_ref, i))
    def _():
        wdb[...] = wd_ref[...].astype(BF16)

    @pl.when(i < na_ref[0])
    def _():
        y_ref[...] = jnp.dot(h_ref[...], wdb[...], preferred_element_type=F32)

    @pl.when(i >= na_ref[0])
    def _():
        y_ref[...] = jnp.zeros_like(y_ref)


def _moe_ffn(xs, block_e, n_active, w_g, w_u, w_d, l):
    P, D = xs.shape
    F = w_g.shape[3]
    nb = P // MOE_BLK
    up_spec = pltpu.PrefetchScalarGridSpec(
        num_scalar_prefetch=2,
        grid=(F // MOE_FC, nb),
        in_specs=[pl.BlockSpec((MOE_BLK, D), lambda j, i, be, na: (i, 0)),
                  pl.BlockSpec((None, None, D, MOE_FC), lambda j, i, be, na: (l, be[i], 0, j)),
                  pl.BlockSpec((None, None, D, MOE_FC), lambda j, i, be, na: (l, be[i], 0, j))],
        out_specs=pl.BlockSpec((MOE_BLK, MOE_FC), lambda j, i, be, na: (i, j)),
        scratch_shapes=[pltpu.VMEM((D, MOE_FC), BF16), pltpu.VMEM((D, MOE_FC), BF16)],
    )
    hmid = pl.pallas_call(
        _moe_up_kernel, grid_spec=up_spec,
        out_shape=jax.ShapeDtypeStruct((P, F), BF16),
        compiler_params=_cparams(), name="moe_up",
    )(block_e, n_active, xs, w_g, w_u)
    down_spec = pltpu.PrefetchScalarGridSpec(
        num_scalar_prefetch=2,
        grid=(D // MOE_DC, nb),
        in_specs=[pl.BlockSpec((MOE_BLK, F), lambda j, i, be, na: (i, 0)),
                  pl.BlockSpec((None, None, F, MOE_DC), lambda j, i, be, na: (l, be[i], 0, j))],
        out_specs=pl.BlockSpec((MOE_BLK, MOE_DC), lambda j, i, be, na: (i, j)),
        scratch_shapes=[pltpu.VMEM((F, MOE_DC), BF16)],
    )
    return pl.pallas_call(
        _moe_down_kernel, grid_spec=down_spec,
        out_shape=jax.ShapeDtypeStruct((P, D), F32),
        compiler_params=_cparams(), name="moe_down",
    )(block_e, n_active, hmid, w_d)


def _route(logits, b_router, valid_row):
    E = b_router.shape[0]
    assert TOP_K == 2
    epg = E // N_GROUPS
    probs = jax.nn.softmax(logits + b_router.astype(F32), axis=-1)
    pg = probs.reshape(-1, N_GROUPS, epg)
    lane = jnp.arange(epg, dtype=jnp.int32)
    m1 = jnp.max(pg, axis=-1, keepdims=True)
    i1 = jnp.min(jnp.where(pg == m1, lane, epg), axis=-1, keepdims=True)
    rest = jnp.where(lane == i1, -jnp.inf, pg)
    m2 = jnp.max(rest, axis=-1, keepdims=True)
    i2 = jnp.min(jnp.where(rest == m2, lane, epg), axis=-1, keepdims=True)
    group_score = (m1 + m2)[..., 0]
    g_sel = jnp.argmax(group_score, axis=-1)
    pick = (jnp.arange(N_GROUPS, dtype=jnp.int32)[None, :] == g_sel[:, None])[..., None]
    vals = jnp.sum(jnp.where(pick, jnp.concatenate([m1, m2], axis=-1), 0.0), axis=1)
    idx = jnp.sum(jnp.where(pick, jnp.concatenate([i1, i2], axis=-1), 0), axis=1)
    expert_idx = (g_sel[:, None] * epg + idx).astype(jnp.int32)
    gates = vals / jnp.sum(vals, axis=-1, keepdims=True)
    expert_idx = jnp.where(valid_row[:, None], expert_idx, E)
    gates = jnp.where(valid_row[:, None], gates, 0.0)
    return expert_idx, gates


def _dispatch(expert_idx, n_valid, E):
    M = expert_idx.shape[0]
    MK = M * TOP_K
    flat_e = expert_idx.reshape(MK)
    flat_tok = jnp.repeat(jnp.arange(M, dtype=jnp.int32), TOP_K)
    onehot = (flat_e[:, None] == jnp.arange(E, dtype=jnp.int32)[None, :]).astype(jnp.int32)
    oh3 = onehot.reshape(MK // LANES, LANES, E)
    within = jnp.cumsum(oh3, axis=1)
    blk_tot = within[:, -1, :]
    blk_off = jnp.cumsum(blk_tot, axis=0) - blk_tot
    rank = jnp.sum((within + blk_off[:, None, :] - 1) * oh3, axis=-1).reshape(MK)
    counts = jnp.sum(blk_tot, axis=0)
    padded = (counts + MOE_BLK - 1) // MOE_BLK * MOE_BLK
    pad_end = jnp.cumsum(padded)
    pad_start = pad_end - padded
    nb = (n_valid * TOP_K + E * (MOE_BLK - 1)) // MOE_BLK + 1
    P = nb * MOE_BLK
    seg_start = jnp.sum(onehot * pad_start[None, :], axis=-1)
    dest = jnp.where(flat_e < E, seg_start + rank, P)
    slot_tok = jnp.zeros((P,), jnp.int32).at[dest].set(flat_tok, mode="drop")
    blk_row0 = jnp.arange(nb, dtype=jnp.int32) * MOE_BLK
    block_e = jnp.minimum(jnp.sum(blk_row0[:, None] >= pad_end[None, :], axis=-1), E - 1).astype(jnp.int32)
    n_active = (pad_end[-1:] // MOE_BLK).astype(jnp.int32)
    pair_dest = jnp.minimum(dest, P - 1).reshape(M, TOP_K)
    return slot_tok, block_e, n_active, pair_dest


def _post_moe_kernel(x_ref, y0_ref, y1_ref, gw_ref, gtp_ref, gts_ref, *rest, npb, with_norm):
    i = pl.program_id(0)
    gt = _pick_mod(i, npb, gtp_ref, gts_ref)
    gw = gw_ref[...]
    y = gw[:, 0:1] * y0_ref[...].astype(F32) + gw[:, 1:2] * y1_ref[...].astype(F32)
    x2 = x_ref[...] + gt * y
    if with_norm:
        g_ref, shp_ref, shs_ref, scp_ref, scs_ref, x2_ref, h_ref = rest
        x2_ref[...] = x2
        sh = _pick_mod(i, npb, shp_ref, shs_ref)
        sc = _pick_mod(i, npb, scp_ref, scs_ref)
        h_ref[...] = (_rms(x2, g_ref[...]) * (1.0 + sc) + sh).astype(BF16)
    else:
        (x2_ref,) = rest
        x2_ref[...] = x2


def _post_moe(rows, x1, y0, y1, gates_pad, l, modp, mods, g_next=None, modp_next=None, mods_next=None):
    M, D = x1.shape
    with_norm = g_next is not None
    gtp, gts = _mod_specs(rows, D, 5)
    row_spec = pl.BlockSpec((ROW_BLK, D), lambda i: (i, 0))
    in_specs = [row_spec, row_spec, row_spec, pl.BlockSpec((ROW_BLK, LANES), lambda i: (i, 0)), gtp, gts]
    args = [x1, y0, y1, gates_pad, modp, mods]
    out_specs = [row_spec]
    out_shape = [jax.ShapeDtypeStruct((M, D), F32)]
    if with_norm:
        shp, shs = _mod_specs(rows, D, 0)
        scp, scs = _mod_specs(rows, D, 1)
        in_specs += [pl.BlockSpec((None, 1, D), lambda i: (l + 1, 0, 0)), shp, shs, scp, scs]
        args += [g_next, modp_next, mods_next, modp_next, mods_next]
        out_specs.append(row_spec)
        out_shape.append(jax.ShapeDtypeStruct((M, D), BF16))
    return pl.pallas_call(
        functools.partial(_post_moe_kernel, npb=rows.npb, with_norm=with_norm),
        grid=(rows.nblk,),
        in_specs=in_specs, out_specs=out_specs, out_shape=out_shape,
        compiler_params=_cparams(),
        name="post_moe",
    )(*args)


def kernel(x_prompt, x_sample, c_prompt, c_sample, state_gla, cache_swa_k, cache_swa_v, w_ada, b_ada, g_norm1, g_norm2, w_in, w_gate_up, b_gate, g_gla_out, g_q_norm, g_k_norm, sinks, w_out, w_router, b_router, w_e_gate, w_e_up, w_e_down):
    B, T, D = x_prompt.shape
    SB, ST, _ = x_sample.shape
    L = w_ada.shape[0]
    H = GLA_HEADS
    dk = state_gla.shape[3]
    dv = state_gla.shape[4]
    nbuf, kvh, hd = cache_swa_k.shape[2:]
    E = w_router.shape[1]
    rows = _Rows(B, T, SB, ST)
    M = rows.M

    gla_cols = 2 * H * dk + 2 * H * dv
    swa_col0 = gla_cols + GATE_RANK
    swa_cols = w_in.shape[2] - swa_col0

    xs_pad = jnp.pad(x_sample, ((0, 0), (0, SPAD - ST), (0, 0))).reshape(SB * SPAD, D)
    x = jnp.concatenate([x_prompt.reshape(B * T, D), xs_pad], axis=0)
    row_id = jnp.arange(M, dtype=jnp.int32)
    valid_row = (row_id < rows.n_prompt) | ((row_id - rows.n_prompt) % SPAD < ST)
    n_valid = rows.n_prompt + SB * ST

    mod = _ada_mod(jnp.concatenate([c_prompt, c_sample], axis=0), w_ada, b_ada)
    modp_all = mod[:, :B].reshape(L, B, 1, 6 * D)
    mods_all = mod[:, B:]

    wgu_pad = jnp.pad(w_gate_up, ((0, 0), (0, LANES - GATE_RANK), (0, 0))).astype(BF16)
    bg3 = b_gate.reshape(L, 1, H * dk)
    go3 = g_gla_out.reshape(L, 1, dv)
    gq3 = g_q_norm.reshape(L, 1, hd)
    gk3 = g_k_norm.reshape(L, 1, hd)
    g1 = g_norm1.reshape(L, 1, D)
    g2 = g_norm2.reshape(L, 1, D)
    wr = jnp.pad(w_router, ((0, 0), (0, LANES - E))).astype(BF16)
    tm = MM_ROWS
    tail_blk0 = (rows.n_prompt - (M - tm)) // SPAD
    assert M % tm == 0 and M - tm <= rows.n_prompt and (rows.n_prompt - (M - tm)) % SPAD == 0
    ck =cache_swa_k.reshape(L, SB, nbuf, kvh * hd)
    cv = cache_swa_v.reshape(L, SB, nbuf, kvh * hd)

    h = _norm_mod(rows, x, g1, 0, modp_all[0], mods_all[0])
    kp_l, vp_l, ks_l, vs_l = [], [], [], []
    s_p = s_s = None
    for l in range(L):
        modp, mods = modp_all[l], mods_all[l]
        proj_a, proj_a_s = _matmul([h], w_in, l, 0, gla_cols, 512, tm, BF16, "proj_gla", with_tail=True)
        (glr,) = _matmul([h], w_in, l, gla_cols // LANES, LANES, LANES, tm, BF16, "proj_gate_lr")
        w_swa = w_in[l, :, swa_col0:].astype(BF16)
        proj_b, proj_b_s = _matmul([h], w_swa, None, 0, swa_cols, 512, tm, BF16, "proj_swa", with_tail=True)

        o_gla, s_p = _gla(proj_a, glr, wgu_pad, bg3, go3, l, M=M, L=L, n_seq=B, n_chunks=T // GLA_CHUNK,
                          C=GLA_CHUNK, valid=GLA_CHUNK, row_blk0=0, stack=s_p)
        o_gla, s_s = _gla(proj_a_s, glr, wgu_pad, bg3, go3, l, M=M, L=L, n_seq=SB, n_chunks=1, C=SPAD, valid=ST,
                          row_blk0=rows.n_prompt // SPAD, in_row_blk0=tail_blk0, s0=state_gla, o_prev=o_gla,
                          stack=s_s)
        o_swa, k_p, v_p = _swa_prompt(proj_b, sinks[l], gq3, gk3, l, B=B, T=T, kvh=kvh)
        o_swa, k_s, v_s = _swa_sample(proj_b_s, o_swa, ck, cv, sinks[l], gq3, gk3, l, SB=SB, ST=ST, kvh=kvh,
                                      row_blk0=rows.n_prompt // SPAD, in_row_blk0=tail_blk0)

        (mix,) = _matmul([o_gla, o_swa], w_out, l, 0, D, 512, tm, F32, "out_proj")
        x1, h2, logits = _post_attn(rows, x, mix, g2, l, modp, mods, wr)

        expert_idx, gates = _route(logits[:, :E], b_router, valid_row)
        slot_tok, block_e, n_active, pair_dest = _dispatch(expert_idx, n_valid, E)
        ys = _moe_ffn(h2[slot_tok], block_e, n_active, w_e_gate, w_e_up, w_e_down, l)
        y0 = ys[pair_dest[:, 0]]
        y1 = ys[pair_dest[:, 1]]
        gates_pad = jnp.pad(gates, ((0, 0), (0, LANES - TOP_K)))
        if l + 1 < L:
            x, h = _post_moe(rows, x1, y0, y1, gates_pad, l, modp, mods, g1, modp_all[l + 1], mods_all[l + 1])
        else:
            (x,) = _post_moe(rows, x1, y0, y1, gates_pad, l, modp, mods)

        kp_l.append(k_p.reshape(B, WINDOW, kvh, hd))
        vp_l.append(v_p.reshape(B, WINDOW, kvh, hd))
        ks_l.append(k_s.reshape(SB, nbuf, kvh, hd))
        vs_l.append(v_s.reshape(SB, nbuf, kvh, hd))

    y_prompt = x[:rows.n_prompt].reshape(B, T, D)
    y_sample = x[rows.n_prompt:].reshape(SB, SPAD, D)[:, :ST]
    return (y_prompt, y_sample, s_p, jnp.stack(kp_l), jnp.stack(vp_l),
            s_s, jnp.stack(ks_l), jnp.stack(vs_l))
```

```python
import functools

import jax
import jax.numpy as jnp
from jax import lax
from jax.experimental import pallas as pl
from jax.experimental.pallas import tpu as pltpu

F32 = jnp.float32
BF16 = jnp.bfloat16

GLA_HEADS = 4
GATE_RANK = 16
GATE_TEMP = 16.0
SWA_HEAD_DIM = 64
SWA_GROUP = 8
WINDOW = 128
N_GROUPS = 4
TOP_K = 2
EPS = 1e-6
NEG_INF = -1e30

LANES = 128
BF16_SUBLANES = 16
VMEM_LIMIT = 48 * 1024 * 1024

SPAD = BF16_SUBLANES
ROW_BLK = 128
GLA_CHUNK = 128
MM_ROWS = 1088
MM_XPOSE = 512
MOE_BLK = 256
MOE_FC = 512
MOE_DC = 2048


def _cparams(**kw):
    return pltpu.CompilerParams(vmem_limit_bytes=VMEM_LIMIT, **kw)


def _ada_kernel(c_ref, w_ref, b_ref, o_ref):
    c = c_ref[...]
    a = (c * (1.0 / (1.0 + jnp.exp(-c)))).astype(BF16)
    o_ref[...] = jnp.dot(a, w_ref[...].astype(BF16), preferred_element_type=F32) + b_ref[...]


def _ada_mod(c_all, w_ada, b_ada):
    L, D, N = w_ada.shape
    R = c_all.shape[0]
    tn = 512
    return pl.pallas_call(
        _ada_kernel,
        grid=(L, N // tn),
        in_specs=[pl.BlockSpec((R, D), lambda l, j: (0, 0)),
                  pl.BlockSpec((None, D, tn), lambda l, j: (l, 0, j)),
                  pl.BlockSpec((None, 1, tn), lambda l, j: (l, 0, j))],
        out_specs=pl.BlockSpec((None, R, tn), lambda l, j: (l, 0, j)),
        out_shape=jax.ShapeDtypeStruct((L, R, N), F32),
        compiler_params=_cparams(),
        name="ada_mod",
    )(c_all, w_ada, b_ada.reshape(L, 1, N))


class _Rows:
    def __init__(self, B, T, SB, ST):
        self.B, self.T, self.SB, self.ST = B, T, SB, ST
        self.n_prompt = B * T
        self.n_sample = SB * SPAD
        self.M = self.n_prompt + self.n_sample
        assert T % ROW_BLK == 0 and self.n_sample % ROW_BLK == 0 and ST <= SPAD
        self.npb = self.n_prompt // ROW_BLK
        self.nsb = self.n_sample // ROW_BLK
        self.nblk = self.npb + self.nsb
        self.blk_per_seq = T // ROW_BLK


def _mod_specs(rows, D, col):
    npb, bps, B = rows.npb, rows.blk_per_seq, rows.B
    p_spec = pl.BlockSpec((None, 1, D), lambda i, *_: (jnp.minimum(i // bps, B - 1), 0, col))
    s_spec = pl.BlockSpec((ROW_BLK // SPAD, D), lambda i, *_: (jnp.maximum(i - npb, 0), col))
    return p_spec, s_spec


def _pick_mod(i, npb, p_ref, s_ref):
    s = s_ref[...]
    s_rows = jnp.concatenate(
        [jnp.broadcast_to(s[r:r + 1, :], (SPAD, s.shape[1])) for r in range(s.shape[0])], axis=0)
    return jnp.where(i < npb, p_ref[...], s_rows)


def _rms(x, g):
    return x * lax.rsqrt(jnp.mean(x * x, axis=-1, keepdims=True) + EPS) * g


def _norm_mod_kernel(x_ref, g_ref, shp_ref, shs_ref, scp_ref, scs_ref, h_ref, *, npb):
    i = pl.program_id(0)
    y = _rms(x_ref[...], g_ref[...])
    sh = _pick_mod(i, npb, shp_ref, shs_ref)
    sc = _pick_mod(i, npb, scp_ref, scs_ref)
    h_ref[...] = (y * (1.0 + sc) + sh).astype(BF16)


def _norm_mod(rows, x, g, l, modp, mods):
    M, D = x.shape
    shp, shs = _mod_specs(rows, D, 0)
    scp, scs = _mod_specs(rows, D, 1)
    row_spec = pl.BlockSpec((ROW_BLK, D), lambda i: (i, 0))
    return pl.pallas_call(
        functools.partial(_norm_mod_kernel, npb=rows.npb),
        grid=(rows.nblk,),
        in_specs=[row_spec, pl.BlockSpec((None, 1, D), lambda i: (l, 0, 0)), shp, shs, scp, scs],
        out_specs=row_spec,
        out_shape=jax.ShapeDtypeStruct((M, D), BF16),
        compiler_params=_cparams(),
        name="norm_mod",
    )(x, g, modp, mods, modp, mods)


def _mm_kernel(*refs, n_lhs, with_tail, w_is_nk):
    x_refs, w_ref, o_ref = refs[:n_lhs], refs[n_lhs], refs[n_lhs + 1]
    wbf_ref = refs[-1]

    @pl.when(pl.program_id(1) == 0)
    def _():
        if w_is_nk:
            K = wbf_ref.shape[0]
            for k0 in range(0, K, MM_XPOSE):
                wbf_ref[k0:k0 + MM_XPOSE, :] = w_ref[:, k0:k0 + MM_XPOSE].T.astype(BF16)
        else:
            wbf_ref[...] = w_ref[...].astype(BF16)

    acc = None
    k0 = 0
    for x_ref in x_refs:
        kk = x_ref.shape[1]
        part = jnp.dot(x_ref[...], wbf_ref[k0:k0 + kk, :], preferred_element_type=F32)
        acc = part if acc is None else acc + part
        k0 += kk
    o_ref[...] = acc.astype(o_ref.dtype)
    if with_tail:
        @pl.when(pl.program_id(1) == pl.num_programs(1) - 1)
        def _():
            refs[n_lhs + 2][...] = acc


def _matmul(xs, w, w_lead, col_blk0, n_cols, tn, tm, out_dtype, name, with_tail=False, w_is_nk=False):
    M = xs[0].shape[0]
    K = sum(x.shape[1] for x in xs)
    if w_is_nk:
        assert K % MM_XPOSE == 0
        if w.ndim == 3:
            w_spec = pl.BlockSpec((None, tn, K), lambda j, i: (w_lead, col_blk0 + j, 0))
        else:
            w_spec = pl.BlockSpec((tn, K), lambda j, i: (col_blk0 + j, 0))
    elif w.ndim == 3:
        w_spec = pl.BlockSpec((None, K, tn), lambda j, i: (w_lead, 0, col_blk0 + j))
    else:
        w_spec = pl.BlockSpec((K, tn), lambda j, i: (0, col_blk0 + j))
    x_specs = [pl.BlockSpec((tm, x.shape[1]), lambda j, i: (i, 0)) for x in xs]
    out_specs = [pl.BlockSpec((tm, tn), lambda j, i: (i, j))]
    out_shape = [jax.ShapeDtypeStruct((M, n_cols), out_dtype)]
    if with_tail:
        out_specs.append(pl.BlockSpec((tm, tn), lambda j, i: (0, j)))
        out_shape.append(jax.ShapeDtypeStruct((tm, n_cols), F32))
    return pl.pallas_call(
        functools.partial(_mm_kernel, n_lhs=len(xs), with_tail=with_tail, w_is_nk=w_is_nk),
        grid=(n_cols // tn, M // tm),
        in_specs=x_specs + [w_spec],
        out_specs=out_specs,
        out_shape=out_shape,
        scratch_shapes=[pltpu.VMEM((K, tn), BF16)],
        compiler_params=_cparams(),
        name=name,
    )(*xs, w)


def _gla_kernel(*refs, C, valid, has_s0, dk, dv):
    q_ref, k_ref, v_ref, gg_ref, glr_ref, wgu_ref, bg_ref, go_ref = refs[:8]
    s0_ref = refs[8] if has_s0 else None
    o_ref, sf_ref, S = refs[-3:]
    c = pl.program_id(2)

    @pl.when(c == 0)
    def _():
        if has_s0:
            S[...] = s0_ref[...]
        else:
            S[...] = jnp.zeros_like(S)

    z = jnp.dot(glr_ref[...], wgu_ref[...], preferred_element_type=F32) + bg_ref[...]
    la = (jnp.minimum(z, 0.0) - jnp.log(1.0 + jnp.exp(-jnp.abs(z)))) * (1.0 / GATE_TEMP)
    q = q_ref[...].astype(F32) * (dk ** -0.5)
    k = k_ref[...].astype(F32)
    if valid < C:
        live = lax.broadcasted_iota(jnp.int32, (C, 1), 0) < valid
        la = jnp.where(live, la, 0.0)
        k = jnp.where(live, k, 0.0)

    la_hi = la.astype(BF16).astype(F32)
    la_lo = (la - la_hi).astype(BF16).astype(F32)
    ri = lax.broadcasted_iota(jnp.int32, (C, C), 0)
    ci = lax.broadcasted_iota(jnp.int32, (C, C), 1)
    tril = ri >= ci
    trilf = tril.astype(F32)
    b = (jnp.dot(trilf, la_hi, preferred_element_type=F32)
         + jnp.dot(trilf, la_lo, preferred_element_type=F32))

    q_dec = (q * jnp.exp(b)).astype(BF16)
    if C > SPAD:
        r = b[C // 2:C // 2 + 1, :]
        q_in = (q * jnp.exp(b - r)).astype(BF16)
        k_in = (k * jnp.exp(r - b)).astype(BF16)
    else:
        q_in = q_dec
        k_in = (k * jnp.exp(-b)).astype(BF16)
    att = lax.dot_general(q_in, k_in, (((1,), (1,)), ((), ())), preferred_element_type=F32)
    att = jnp.where(tril, att, 0.0)

    s_old = S[...]
    v = v_ref[...].astype(BF16)
    o = (jnp.dot(att.astype(BF16), v, preferred_element_type=F32)
         + jnp.dot(q_dec, s_old.astype(BF16), preferred_element_type=F32))

    b_end = b[valid - 1:valid, :]
    k_end = (k * jnp.exp(b_end - b)).astype(BF16)
    tn_dims = (((0,), (0,)), ((), ()))
    upd = lax.dot_general(k_end, v, tn_dims, preferred_element_type=F32)
    ones = jnp.ones((C, LANES), F32)
    b_col = (lax.dot_general(la_hi, ones, tn_dims, preferred_element_type=F32)
             + lax.dot_general(la_lo, ones, tn_dims, preferred_element_type=F32))
    dec = jnp.exp(b_col)
    S[...] = jnp.concatenate([dec] * (dv // LANES), axis=1) * s_old + upd

    gg = gg_ref[...].astype(F32)
    on = _rms(o, go_ref[...])
    o_ref[...] = (on * (gg * (1.0 / (1.0 + jnp.exp(-gg))))).astype(BF16)

    @pl.when(c == pl.num_programs(2) - 1)
    def _():
        sf_ref[...] = S[...]


def _gla(proj_a, glr, wgu, bg, go, l, *, M, L, n_seq, n_chunks, C, valid, row_blk0, in_row_blk0=None,
         s0=None, o_prev=None, stack=None):
    H = GLA_HEADS
    dk = wgu.shape[2] // H
    dv = go.shape[2]
    has_s0 = s0 is not None
    if in_row_blk0 is None:
        in_row_blk0 = row_blk0

    def rb(b, c):
        return row_blk0 + b * n_chunks + c

    def rbi(b, c):
        return in_row_blk0 + b * n_chunks + c

    in_specs = [
        pl.BlockSpec((C, dk), lambda b, h, c: (rbi(b, c), h)),
        pl.BlockSpec((C, dk), lambda b, h, c: (rbi(b, c), H + h)),
        pl.BlockSpec((C, dv), lambda b, h, c: (rbi(b, c), (2 * H * dk) // dv + h)),
        pl.BlockSpec((C, dv), lambda b, h, c: (rbi(b, c), (2 * H * dk) // dv + H + h)),
        pl.BlockSpec((C, LANES), lambda b, h, c: (rb(b, c), 0)),
        pl.BlockSpec((None, LANES, dk), lambda b, h, c: (l, 0, h)),
        pl.BlockSpec((None, 1, dk), lambda b, h, c: (l, 0, h)),
        pl.BlockSpec((None, 1, dv), lambda b, h, c: (l, 0, 0)),
    ]
    args = [proj_a, proj_a, proj_a, proj_a, glr, wgu, bg, go]
    aliases = {}
    if has_s0:
        in_specs += [pl.BlockSpec((None, None, None, dk, dv), lambda b, h, c: (l, b, h, 0, 0)),
                     pl.BlockSpec(memory_space=pl.ANY)]
        args += [s0, o_prev]
        aliases = {len(args) - 1: 0}
    if stack is not None:
        in_specs.append(pl.BlockSpec(memory_space=pl.ANY))
        args.append(stack)
        aliases[len(args) - 1] = 1
    return pl.pallas_call(
        functools.partial(_gla_kernel, C=C, valid=valid, has_s0=has_s0, dk=dk, dv=dv),
        grid=(n_seq, H, n_chunks),
        in_specs=in_specs,
        out_specs=[pl.BlockSpec((C, dv), lambda b, h, c: (rb(b, c), h)),
                   pl.BlockSpec((None, None, None, dk, dv), lambda b, h, c: (l, b, h, 0, 0))],
        out_shape=[jax.ShapeDtypeStruct((M, H * dv), BF16),
                   jax.ShapeDtypeStruct((L, n_seq, H, dk, dv), F32)],
        scratch_shapes=[pltpu.VMEM((dk, dv), F32)],
        input_output_aliases=aliases,
        compiler_params=_cparams(),
        name="gla_sample" if has_s0 else "gla_prompt",
    )(*args)


def _head_norm(x, g):
    return x * lax.rsqrt(jnp.mean(x * x, axis=-1, keepdims=True) + EPS) * g


def _segment_rms_inv(x, seg):
    R, Wd = x.shape
    assert Wd // seg <= LANES
    sel = (lax.broadcasted_iota(jnp.int32, (Wd, LANES), 0) // seg
           == lax.broadcasted_iota(jnp.int32, (Wd, LANES), 1)).astype(F32).astype(BF16)
    sel_t = (lax.broadcasted_iota(jnp.int32, (LANES, Wd), 0)
             == lax.broadcasted_iota(jnp.int32, (LANES, Wd), 1) // seg).astype(F32).astype(BF16)

    def split(a):
        hi = a.astype(BF16)
        return hi, (a - hi.astype(F32)).astype(BF16)

    sq_hi, sq_lo = split(x * x)
    sums = (jnp.dot(sq_hi, sel, preferred_element_type=F32)
            + jnp.dot(sq_lo, sel, preferred_element_type=F32))
    inv_hi, inv_lo = split(lax.rsqrt(sums * (1.0 / seg) + EPS))
    return (jnp.dot(inv_hi, sel_t, preferred_element_type=F32)
            + jnp.dot(inv_lo, sel_t, preferred_element_type=F32))


def _softmax_sink_pv(parts, sink):
    m = sink
    for s, _ in parts:
        m = jnp.maximum(m, jnp.max(s, axis=-1, keepdims=True))
    den = jnp.exp(sink - m)
    es = []
    for s, _ in parts:
        e = jnp.exp(s - m)
        den = den + jnp.sum(e, axis=-1, keepdims=True)
        es.append(e)
    inv = 1.0 / den
    acc = None
    for e, (_, v) in zip(es, parts):
        pv = jnp.dot((e * inv).astype(BF16), v, preferred_element_type=F32)
        acc = pv if acc is None else acc + pv
    return acc


NT_DIMS = (((1,), (1,)), ((), ()))


def _swa_prompt_kernel(sink_ref, q_ref, kp_ref, kc_ref, vp_ref, vc_ref, gq_ref, gk_ref,
                       o_ref, ko_ref, vo_ref, *, kvh):
    i = pl.program_id(1)
    hd = SWA_HEAD_DIM
    W = WINDOW
    G = SWA_GROUP
    r = lax.broadcasted_iota(jnp.int32, (G * W, 2 * W), 0) % W
    j = lax.broadcasted_iota(jnp.int32, (G * W, 2 * W), 1)
    mask = (j >= r) & (j <= r + W) & ((j >= W) | (i > 0))
    head_of_row = lax.broadcasted_iota(jnp.int32, (G * W, 1), 0) // W
    gk = gk_ref[...]
    q = q_ref[...].astype(F32)
    gq_row = jnp.concatenate([gq_ref[...]] * (q.shape[1] // hd), axis=1)
    qn = (q * _segment_rms_inv(q, hd) * gq_row * (hd ** -0.5)).astype(BF16)
    kp = kp_ref[...].astype(F32)
    kc = kc_ref[...].astype(F32)
    k_new = []
    outs = []
    for kv in range(kvh):
        sl = slice(kv * hd, (kv + 1) * hd)
        kcn = _head_norm(kc[:, sl], gk)
        k_new.append(kcn)
        kcat = jnp.concatenate([_head_norm(kp[:, sl], gk), kcn], axis=0).astype(BF16)
        vcat = jnp.concatenate([vp_ref[:, sl], vc_ref[:, sl]], axis=0)
        h0 = kv * G
        qg = jnp.concatenate([qn[:, (h0 + g) * hd:(h0 + g + 1) * hd] for g in range(G)], axis=0)
        sink = jnp.zeros((G * W, 1), F32)
        for g in range(G):
            sink = jnp.where(head_of_row == g, sink_ref[h0 + g], sink)
        s = lax.dot_general(qg, kcat, NT_DIMS, preferred_element_type=F32)
        s = jnp.where(mask, s, NEG_INF)
        og = _softmax_sink_pv([(s, vcat)], sink)
        outs += [og[g * W:(g + 1) * W, :] for g in range(G)]
    o_ref[...] = jnp.concatenate(outs, axis=1).astype(BF16)
    ko_ref[...] = jnp.concatenate(k_new, axis=1)
    vo_ref[...] = vc_ref[...].astype(F32)


def _swa_prompt(proj_b, sinks_l, gq, gk, l, *, B, T, kvh):
    M = proj_b.shape[0]
    hd = SWA_HEAD_DIM
    qw = kvh * SWA_GROUP * hd
    kw = kvh * hd
    nb = T // WINDOW
    kblk = qw // kw

    def cur(b, i, s):
        return b * nb + i

    def prev(b, i, s):
        return b * nb + jnp.maximum(i - 1, 0)

    grid_spec = pltpu.PrefetchScalarGridSpec(
        num_scalar_prefetch=1,
        grid=(B, nb),
        in_specs=[
            pl.BlockSpec((WINDOW, qw), lambda b, i, s: (cur(b, i, s), 0)),
            pl.BlockSpec((WINDOW, kw), lambda b, i, s: (prev(b, i, s), kblk)),
            pl.BlockSpec((WINDOW, kw), lambda b, i, s: (cur(b, i, s), kblk)),
            pl.BlockSpec((WINDOW, kw), lambda b, i, s: (prev(b, i, s), kblk + 1)),
            pl.BlockSpec((WINDOW, kw), lambda b, i, s: (cur(b, i, s), kblk + 1)),
            pl.BlockSpec((None, 1, hd), lambda b, i, s: (l, 0, 0)),
            pl.BlockSpec((None, 1, hd), lambda b, i, s: (l, 0, 0)),
        ],
        out_specs=[
            pl.BlockSpec((WINDOW, qw), lambda b, i, s: (cur(b, i, s), 0)),
            pl.BlockSpec((None, WINDOW, kw), lambda b, i, s: (b, 0, 0)),
            pl.BlockSpec((None, WINDOW, kw), lambda b, i, s: (b, 0, 0)),
        ],
    )
    return pl.pallas_call(
        functools.partial(_swa_prompt_kernel, kvh=kvh),
        grid_spec=grid_spec,
        out_shape=[jax.ShapeDtypeStruct((M, qw), BF16),
                   jax.ShapeDtypeStruct((B, WINDOW, kw), F32),
                   jax.ShapeDtypeStruct((B, WINDOW, kw), F32)],
        compiler_params=_cparams(),
        name="swa_prompt",
    )(sinks_l, proj_b, proj_b, proj_b, proj_b, proj_b, gq, gk)


def _swa_sample_kernel(sink_ref, q_ref, kn_ref, vn_ref, kc_ref, vc_ref, gq_ref, gk_ref, _o_prev,
                       o_ref, ko_ref, vo_ref, *, kvh, st, nbuf):
    hd = SWA_HEAD_DIM
    G = SWA_GROUP
    r1 = lax.broadcasted_iota(jnp.int32, (G * SPAD, nbuf), 0) % SPAD
    j1 = lax.broadcasted_iota(jnp.int32, (G * SPAD, nbuf), 1)
    mask_buf = (nbuf + r1 - j1 >= 0) & (nbuf + r1 - j1 <= WINDOW)
    r2 = lax.broadcasted_iota(jnp.int32, (G * SPAD, SPAD), 0) % SPAD
    t2 = lax.broadcasted_iota(jnp.int32, (G * SPAD, SPAD), 1)
    mask_new = (t2 <= r2) & (t2 < st)
    head_of_row = lax.broadcasted_iota(jnp.int32, (G * SPAD, 1), 0) // SPAD
    gq = gq_ref[...]
    gk = gk_ref[...]
    kn = kn_ref[...].astype(F32)
    k_new = []
    outs = []
    for kv in range(kvh):
        sl = slice(kv * hd, (kv + 1) * hd)
        knn = _head_norm(kn[:, sl], gk)
        k_new.append(knn)
        kbuf = kc_ref[:, sl].astype(BF16)
        vbuf = vc_ref[:, sl].astype(BF16)
        knb = knn.astype(BF16)
        vnb = vn_ref[:, sl].astype(BF16)
        h0 = kv * G
        qg = jnp.concatenate(
            [_head_norm(q_ref[:, (h0 + g) * hd:(h0 + g + 1) * hd].astype(F32), gq) for g in range(G)], axis=0)
        qg = (qg * (hd ** -0.5)).astype(BF16)
        sink = jnp.zeros((G * SPAD, 1), F32)
        for g in range(G):
            sink = jnp.where(head_of_row == g, sink_ref[h0 + g], sink)
        s1 = lax.dot_general(qg, kbuf, NT_DIMS, preferred_element_type=F32)
        s2 = lax.dot_general(qg, knb, NT_DIMS, preferred_element_type=F32)
        s1 = jnp.where(mask_buf, s1, NEG_INF)
        s2 = jnp.where(mask_new, s2, NEG_INF)
        og = _softmax_sink_pv([(s1, vbuf), (s2, vnb)], sink)
        outs += [og[g * SPAD:(g + 1) * SPAD, :] for g in range(G)]
    o_ref[...] = jnp.concatenate(outs, axis=1).astype(BF16)
    k_all = jnp.concatenate(k_new, axis=1)
    ko_ref[...] = jnp.concatenate([kc_ref[st:, :], k_all[:st, :]], axis=0)
    vo_ref[...] = jnp.concatenate([vc_ref[st:, :], vn_ref[:st, :].astype(F32)], axis=0)


def _swa_sample(proj_b, o_prev, cache_k, cache_v, sinks_l, gq, gk, l, *, SB, ST, kvh, row_blk0, in_row_blk0):
    M = o_prev.shape[0]
    hd = SWA_HEAD_DIM
    qw = kvh * SWA_GROUP * hd
    kw = kvh * hd
    nbuf = cache_k.shape[2]
    kblk = qw // kw
    grid_spec = pltpu.PrefetchScalarGridSpec(
        num_scalar_prefetch=1,
        grid=(SB,),
        in_specs=[
            pl.BlockSpec((SPAD, qw), lambda b, s: (in_row_blk0 + b, 0)),
            pl.BlockSpec((SPAD, kw), lambda b, s: (in_row_blk0 + b, kblk)),
            pl.BlockSpec((SPAD, kw), lambda b, s: (in_row_blk0 + b, kblk + 1)),
            pl.BlockSpec((None, None, nbuf, kw), lambda b, s: (l, b, 0, 0)),
            pl.BlockSpec((None, None, nbuf, kw), lambda b, s: (l, b, 0, 0)),
            pl.BlockSpec((None, 1, hd), lambda b, s: (l, 0, 0)),
            pl.BlockSpec((None, 1, hd), lambda b, s: (l, 0, 0)),
            pl.BlockSpec(memory_space=pl.ANY),
        ],
        out_specs=[
            pl.BlockSpec((SPAD, qw), lambda b, s: (row_blk0 + b, 0)),
            pl.BlockSpec((None, nbuf, kw), lambda b, s: (b, 0, 0)),
            pl.BlockSpec((None, nbuf, kw), lambda b, s: (b, 0, 0)),
        ],
    )
    return pl.pallas_call(
        functools.partial(_swa_sample_kernel, kvh=kvh, st=ST, nbuf=nbuf),
        grid_spec=grid_spec,
        out_shape=[jax.ShapeDtypeStruct((M, qw), BF16),
                   jax.ShapeDtypeStruct((SB, nbuf, kw), F32),
                   jax.ShapeDtypeStruct((SB, nbuf, kw), F32)],
        input_output_aliases={8: 0},
        compiler_params=_cparams(),
        name="swa_sample",
    )(sinks_l, proj_b, proj_b, proj_b, cache_k, cache_v, gq, gk, o_prev)


def _post_attn_kernel(x_ref, mix_ref, g_ref, gtp_ref, gts_ref, shp_ref, shs_ref, scp_ref, scs_ref,
                      wr_ref, x1_ref, h2_ref, lg_ref, *, npb):
    i = pl.program_id(0)
    gt = _pick_mod(i, npb, gtp_ref, gts_ref)
    x1 = x_ref[...] + gt * mix_ref[...]
    x1_ref[...] = x1
    sh = _pick_mod(i, npb, shp_ref, shs_ref)
    sc = _pick_mod(i, npb, scp_ref, scs_ref)
    h2 = _rms(x1, g_ref[...]) * (1.0 + sc) + sh
    hb = h2.astype(BF16)
    h2_ref[...] = hb
    lg_ref[...] = jnp.dot(hb, wr_ref[...], preferred_element_type=F32)


def _post_attn(rows, x, mix, g, l, modp, mods, wr):
    M, D = x.shape
    gtp, gts = _mod_specs(rows, D, 2)
    shp, shs = _mod_specs(rows, D, 3)
    scp, scs = _mod_specs(rows, D, 4)
    row_spec = pl.BlockSpec((ROW_BLK, D), lambda i: (i, 0))
    wr_spec = pl.BlockSpec((D, LANES), lambda i: (0, 0))
    return pl.pallas_call(
        functools.partial(_post_attn_kernel, npb=rows.npb),
        grid=(rows.nblk,),
        in_specs=[row_spec, row_spec, pl.BlockSpec((None, 1, D), lambda i: (l, 0, 0)),
                  gtp, gts, shp, shs, scp, scs, wr_spec],
        out_specs=[row_spec, row_spec, pl.BlockSpec((ROW_BLK, LANES), lambda i: (i, 0))],
        out_shape=[jax.ShapeDtypeStruct((M, D), F32),
                   jax.ShapeDtypeStruct((M, D), BF16),
                   jax.ShapeDtypeStruct((M, LANES), F32)],
        compiler_params=_cparams(),
        name="post_attn",
    )(x, mix, g, modp, mods, modp, mods, modp, mods, wr)


def _first_block_of_expert(be_ref, i):
    return (i == 0) | (be_ref[i] != be_ref[jnp.maximum(i - 1, 0)])


def _moe_up_kernel(be_ref, na_ref, x_ref, wg_ref, wu_ref, h_ref, wgb, wub):
    i = pl.program_id(1)

    @pl.when(_first_block_of_expert(be_ref, i))
    def _():
        wgb[...] = wg_ref[...].astype(BF16)
        wub[...] = wu_ref[...].astype(BF16)

    @pl.when(i < na_ref[0])
    def _():
        x = x_ref[...]
        a = jnp.dot(x, wgb[...], preferred_element_type=F32)
        u = jnp.dot(x, wub[...], preferred_element_type=F32)
        h_ref[...] = (a * (1.0 / (1.0 + jnp.exp(-a))) * u).astype(BF16)

    @pl.when(i >= na_ref[0])
    def _():
        h_ref[...] = jnp.zeros_like(h_ref)


def _moe_down_kernel(be_ref, na_ref, h_ref, wd_ref, y_ref, wdb):
    i = pl.program_id(1)

    @pl.when(_first_block_of_expert(be_ref, i))
    def _():
        wdb[...] = wd_ref[...].astype(BF16)

    @pl.when(i < na_ref[0])
    def _():
        y_ref[...] = jnp.dot(h_ref[...], wdb[...], preferred_element_type=F32)

    @pl.when(i >= na_ref[0])
    def _():
        y_ref[...] = jnp.zeros_like(y_ref)


def _moe_ffn(xs, block_e, n_active, w_g, w_u, w_d, l):
    P, D = xs.shape
    F = w_g.shape[3]
    nb = P // MOE_BLK
    up_spec = pltpu.PrefetchScalarGridSpec(
        num_scalar_prefetch=2,
        grid=(F // MOE_FC, nb),
        in_specs=[pl.BlockSpec((MOE_BLK, D), lambda j, i, be, na: (i, 0)),
                  pl.BlockSpec((None, None, D, MOE_FC), lambda j, i, be, na: (l, be[i], 0, j)),
                  pl.BlockSpec((None, None, D, MOE_FC), lambda j, i, be, na: (l, be[i], 0, j))],
        out_specs=pl.BlockSpec((MOE_BLK, MOE_FC), lambda j, i, be, na: (i, j)),
        scratch_shapes=[pltpu.VMEM((D, MOE_FC), BF16), pltpu.VMEM((D, MOE_FC), BF16)],
    )
    hmid = pl.pallas_call(
        _moe_up_kernel, grid_spec=up_spec,
        out_shape=jax.ShapeDtypeStruct((P, F), BF16),
        compiler_params=_cparams(), name="moe_up",
    )(block_e, n_active, xs, w_g, w_u)
    down_spec = pltpu.PrefetchScalarGridSpec(
        num_scalar_prefetch=2,
        grid=(D // MOE_DC, nb),
        in_specs=[pl.BlockSpec((MOE_BLK, F), lambda j, i, be, na: (i, 0)),
                  pl.BlockSpec((None, None, F, MOE_DC), lambda j, i, be, na: (l, be[i], 0, j))],
        out_specs=pl.BlockSpec((MOE_BLK, MOE_DC), lambda j, i, be, na: (i, j)),
        scratch_shapes=[pltpu.VMEM((F, MOE_DC), BF16)],
    )
    return pl.pallas_call(
        _moe_down_kernel, grid_spec=down_spec,
        out_shape=jax.ShapeDtypeStruct((P, D), F32),
        compiler_params=_cparams(), name="moe_down",
    )(block_e, n_active, hmid, w_d)


def _route(logits, b_router, valid_row):
    E = b_router.shape[0]
    assert TOP_K == 2
    epg = E // N_GROUPS
    probs = jax.nn.softmax(logits + b_router.astype(F32), axis=-1)
    pg = probs.reshape(-1, N_GROUPS, epg)
    lane = jnp.arange(epg, dtype=jnp.int32)
    m1 = jnp.max(pg, axis=-1, keepdims=True)
    i1 = jnp.min(jnp.where(pg == m1, lane, epg), axis=-1, keepdims=True)
    rest = jnp.where(lane == i1, -jnp.inf, pg)
    m2 = jnp.max(rest, axis=-1, keepdims=True)
    i2 = jnp.min(jnp.where(rest == m2, lane, epg), axis=-1, keepdims=True)
    group_score = (m1 + m2)[..., 0]
    g_sel = jnp.argmax(group_score, axis=-1)
    pick = (jnp.arange(N_GROUPS, dtype=jnp.int32)[None, :] == g_sel[:, None])[..., None]
    vals = jnp.sum(jnp.where(pick, jnp.concatenate([m1, m2], axis=-1), 0.0), axis=1)
    idx = jnp.sum(jnp.where(pick, jnp.concatenate([i1, i2], axis=-1), 0), axis=1)
    expert_idx = (g_sel[:, None] * epg + idx).astype(jnp.int32)
    gates = vals / jnp.sum(vals, axis=-1, keepdims=True)
    expert_idx = jnp.where(valid_row[:, None], expert_idx, E)
    gates = jnp.where(valid_row[:, None], gates, 0.0)
    return expert_idx, gates


def _dispatch(expert_idx, n_valid, E):
    M = expert_idx.shape[0]
    MK = M * TOP_K
    flat_e = expert_idx.reshape(MK)
    flat_tok = jnp.repeat(jnp.arange(M, dtype=jnp.int32), TOP_K)
    onehot = (flat_e[:, None] == jnp.arange(E, dtype=jnp.int32)[None, :]).astype(jnp.int32)
    oh3 = onehot.reshape(MK // LANES, LANES, E)
    within = jnp.cumsum(oh3, axis=1)
    blk_tot = within[:, -1, :]
    blk_off = jnp.cumsum(blk_tot, axis=0) - blk_tot
    rank = jnp.sum((within + blk_off[:, None, :] - 1) * oh3, axis=-1).reshape(MK)
    counts = jnp.sum(blk_tot, axis=0)
    padded = (counts + MOE_BLK - 1) // MOE_BLK * MOE_BLK
    pad_end = jnp.cumsum(padded)
    pad_start = pad_end - padded
    nb = (n_valid * TOP_K + E * (MOE_BLK - 1)) // MOE_BLK + 1
    P = nb * MOE_BLK
    seg_start = jnp.sum(onehot * pad_start[None, :], axis=-1)
    dest = jnp.where(flat_e < E, seg_start + rank, P)
    slot_tok = jnp.zeros((P,), jnp.int32).at[dest].set(flat_tok, mode="drop")
    blk_row0 = jnp.arange(nb, dtype=jnp.int32) * MOE_BLK
    block_e = jnp.minimum(jnp.sum(blk_row0[:, None] >= pad_end[None, :], axis=-1), E - 1).astype(jnp.int32)
    n_active = (pad_end[-1:] // MOE_BLK).astype(jnp.int32)
    pair_dest = jnp.minimum(dest, P - 1).reshape(M, TOP_K)
    return slot_tok, block_e, n_active, pair_dest


def _post_moe_kernel(pd_ref, x_ref, ys_hbm, gw_ref, gtp_ref, gts_ref, *rest, npb, with_norm):
    ybuf, sem = rest[-2:]
    rest = rest[:-2]
    i = pl.program_id(0)
    nsteps = pl.num_programs(0)

    def row_copy(step, slot, r, k):
        d = pd_ref[(step * ROW_BLK + r) * TOP_K + k]
        return pltpu.make_async_copy(ys_hbm.at[d], ybuf.at[slot, k, r], sem.at[slot])

    def issue(step, slot):
        def body(r, carry):
            for k in range(TOP_K):
                row_copy(step, slot, r, k).start()
            return carry
        lax.fori_loop(0, ROW_BLK, body, 0)

    @pl.when(i == 0)
    def _():
        issue(0, 0)

    @pl.when(i + 1 < nsteps)
    def _():
        issue(i + 1, (i + 1) % 2)

    slot = i % 2
    pltpu.make_async_copy(ybuf.at[slot], ybuf.at[slot], sem.at[slot]).wait()

    gt = _pick_mod(i, npb, gtp_ref, gts_ref)
    gw = gw_ref[...]
    y = gw[:, 0:1] * ybuf[slot, 0] + gw[:, 1:2] * ybuf[slot, 1]
    x2 = x_ref[...] + gt * y
    if with_norm:
        g_ref, shp_ref, shs_ref, scp_ref, scs_ref, x2_ref, h_ref = rest
        x2_ref[...] = x2
        sh = _pick_mod(i, npb, shp_ref, shs_ref)
        sc = _pick_mod(i, npb, scp_ref, scs_ref)
        h_ref[...] = (_rms(x2, g_ref[...]) * (1.0 + sc) + sh).astype(BF16)
    else:
        (x2_ref,) = rest
        x2_ref[...] = x2


def _post_moe(rows, x1, ys, pair_dest, gates_pad, l, modp, mods, g_next=None, modp_next=None, mods_next=None):
    M, D = x1.shape
    with_norm = g_next is not None
    gtp, gts = _mod_specs(rows, D, 5)
    row_spec = pl.BlockSpec((ROW_BLK, D), lambda i, *_: (i, 0))
    in_specs = [row_spec, pl.BlockSpec(memory_space=pl.ANY),
                pl.BlockSpec((ROW_BLK, LANES), lambda i, *_: (i, 0)), gtp, gts]
    args = [x1, ys, gates_pad, modp, mods]
    out_specs = [row_spec]
    out_shape = [jax.ShapeDtypeStruct((M, D), F32)]
    if with_norm:
        shp, shs = _mod_specs(rows, D, 0)
        scp, scs = _mod_specs(rows, D, 1)
        in_specs += [pl.BlockSpec((None, 1, D), lambda i, *_: (l + 1, 0, 0)), shp, shs, scp, scs]
        args += [g_next, modp_next, mods_next, modp_next, mods_next]
        out_specs.append(row_spec)
        out_shape.append(jax.ShapeDtypeStruct((M, D), BF16))
    grid_spec = pltpu.PrefetchScalarGridSpec(
        num_scalar_prefetch=1,
        grid=(rows.nblk,),
        in_specs=in_specs, out_specs=out_specs,
        scratch_shapes=[pltpu.VMEM((2, TOP_K, ROW_BLK, D), F32), pltpu.SemaphoreType.DMA((2,))],
    )
    return pl.pallas_call(
        functools.partial(_post_moe_kernel, npb=rows.npb, with_norm=with_norm),
        grid_spec=grid_spec, out_shape=out_shape,
        compiler_params=_cparams(),
        name="post_moe",
    )(pair_dest.reshape(-1), *args)


def kernel(x_prompt, x_sample, c_prompt, c_sample, state_gla, cache_swa_k, cache_swa_v, w_ada, b_ada, g_norm1, g_norm2, w_in, w_gate_up, b_gate, g_gla_out, g_q_norm, g_k_norm, sinks, w_out, w_router, b_router, w_e_gate, w_e_up, w_e_down):
    B, T, D = x_prompt.shape
    SB, ST, _ = x_sample.shape
    L = w_ada.shape[0]
    H = GLA_HEADS
    dk = state_gla.shape[3]
    dv = state_gla.shape[4]
    nbuf, kvh, hd = cache_swa_k.shape[2:]
    E = w_router.shape[1]
    rows = _Rows(B, T, SB, ST)
    M = rows.M

    gla_cols = 2 * H * dk + 2 * H * dv
    swa_col0 = gla_cols + GATE_RANK
    swa_cols = w_in.shape[2] - swa_col0

    xs_pad = jnp.pad(x_sample, ((0, 0), (0, SPAD - ST), (0, 0))).reshape(SB * SPAD, D)
    x = jnp.concatenate([x_prompt.reshape(B * T, D), xs_pad], axis=0)
    row_id = jnp.arange(M, dtype=jnp.int32)
    valid_row = (row_id < rows.n_prompt) | ((row_id - rows.n_prompt) % SPAD < ST)
    n_valid = rows.n_prompt + SB * ST

    mod = _ada_mod(jnp.concatenate([c_prompt, c_sample], axis=0), w_ada, b_ada)
    modp_all = mod[:, :B].reshape(L, B, 1, 6 * D)
    mods_all = mod[:, B:]

    wgu_pad = jnp.pad(w_gate_up, ((0, 0), (0, LANES - GATE_RANK), (0, 0))).astype(BF16)
    bg3 = b_gate.reshape(L, 1, H * dk)
    go3 = g_gla_out.reshape(L, 1, dv)
    gq3 = g_q_norm.reshape(L, 1, hd)
    gk3 = g_k_norm.reshape(L, 1, hd)
    g1 = g_norm1.reshape(L, 1, D)
    g2 = g_norm2.reshape(L, 1, D)
    wr = jnp.pad(w_router, ((0, 0), (0, LANES - E))).astype(BF16)
    w_in_t = jnp.swapaxes(w_in, 1, 2)
    tm = MM_ROWS
    tail_blk0 = (rows.n_prompt - (M - tm)) // SPAD
    assert M % tm == 0 and M - tm <= rows.n_prompt and (rows.n_prompt - (M - tm)) % SPAD == 0
    ck =cache_swa_k.reshape(L, SB, nbuf, kvh * hd)
    cv = cache_swa_v.reshape(L, SB, nbuf, kvh * hd)

    h = _norm_mod(rows, x, g1, 0, modp_all[0], mods_all[0])
    kp_l, vp_l, ks_l, vs_l = [], [], [], []
    s_p = s_s = None
    for l in range(L):
        modp, mods = modp_all[l], mods_all[l]
        proj_a, proj_a_s = _matmul([h], w_in_t, l, 0, gla_cols, 512, tm, BF16, "proj_gla", with_tail=True,
                                   w_is_nk=True)
        (glr,) = _matmul([h], w_in_t, l, gla_cols // LANES, LANES, LANES, tm, BF16, "proj_gate_lr", w_is_nk=True)
        w_swa = w_in_t[l, swa_col0:, :]
        proj_b, proj_b_s = _matmul([h], w_swa, None, 0, swa_cols, 512, tm, BF16, "proj_swa", with_tail=True,
                                   w_is_nk=True)

        o_gla, s_p = _gla(proj_a, glr, wgu_pad, bg3, go3, l, M=M, L=L, n_seq=B, n_chunks=T // GLA_CHUNK,
                          C=GLA_CHUNK, valid=GLA_CHUNK, row_blk0=0, stack=s_p)
        o_gla, s_s = _gla(proj_a_s, glr, wgu_pad, bg3, go3, l, M=M, L=L, n_seq=SB, n_chunks=1, C=SPAD, valid=ST,
                          row_blk0=rows.n_prompt // SPAD, in_row_blk0=tail_blk0, s0=state_gla, o_prev=o_gla,
                          stack=s_s)
        o_swa, k_p, v_p = _swa_prompt(proj_b, sinks[l], gq3, gk3, l, B=B, T=T, kvh=kvh)
        o_swa, k_s, v_s = _swa_sample(proj_b_s, o_swa, ck, cv, sinks[l], gq3, gk3, l, SB=SB, ST=ST, kvh=kvh,
                                      row_blk0=rows.n_prompt // SPAD, in_row_blk0=tail_blk0)

        (mix,) = _matmul([o_gla, o_swa], w_out, l, 0, D, 512, tm, F32, "out_proj")
        x1, h2, logits = _post_attn(rows, x, mix, g2, l, modp, mods, wr)

        expert_idx, gates = _route(logits[:, :E], b_router, valid_row)
        slot_tok, block_e, n_active, pair_dest = _dispatch(expert_idx, n_valid, E)
        ys = _moe_ffn(h2[slot_tok], block_e, n_active, w_e_gate, w_e_up, w_e_down, l)
        gates_pad = jnp.pad(gates, ((0, 0), (0, LANES - TOP_K)))
        if l + 1 < L:
            x, h = _post_moe(rows, x1, ys, pair_dest, gates_pad, l, modp, mods, g1, modp_all[l + 1], mods_all[l + 1])
        else:
            (x,) = _post_moe(rows, x1, ys, pair_dest, gates_pad, l, modp, mods)

        kp_l.append(k_p.reshape(B, WINDOW, kvh, hd))
        vp_l.append(v_p.reshape(B, WINDOW, kvh, hd))
        ks_l.append(k_s.reshape(SB, nbuf, kvh, hd))
        vs_l.append(v_s.reshape(SB, nbuf, kvh, hd))

    y_prompt = x[:rows.n_prompt].reshape(B, T, D)
    y_sample = x[rows.n_prompt:].reshape(SB, SPAD, D)[:, :ST]
    return (y_prompt, y_sample, s_p, jnp.stack(kp_l), jnp.stack(vp_l),
            s_s, jnp.stack(ks_l), jnp.stack(vs_l))
```

```python
import functools

import jax
import jax.numpy as jnp
from jax import lax
from jax.experimental import pallas as pl
from jax.experimental.pallas import tpu as pltpu

F32 = jnp.float32
BF16 = jnp.bfloat16

GLA_HEADS = 4
GATE_RANK = 16
GATE_TEMP = 16.0
SWA_HEAD_DIM = 64
SWA_GROUP = 8
WINDOW = 128
N_GROUPS = 4
TOP_K = 2
EPS = 1e-6
NEG_INF = -1e30

LANES = 128
BF16_SUBLANES = 16
VMEM_LIMIT = 48 * 1024 * 1024

SPAD = BF16_SUBLANES
ROW_BLK = 128
GLA_CHUNK = 128
MM_ROWS = 1088
MM_XPOSE = 512
MOE_BLK = 256
MOE_FC = 512
MOE_DC = 2048


def _cparams(**kw):
    return pltpu.CompilerParams(vmem_limit_bytes=VMEM_LIMIT, **kw)


def _ada_kernel(c_ref, w_ref, b_ref, o_ref):
    c = c_ref[...]
    a = (c * (1.0 / (1.0 + jnp.exp(-c)))).astype(BF16)
    o_ref[...] = jnp.dot(a, w_ref[...].astype(BF16), preferred_element_type=F32) + b_ref[...]


def _ada_mod(c_all, w_ada, b_ada):
    L, D, N = w_ada.shape
    R = c_all.shape[0]
    tn = 512
    return pl.pallas_call(
        _ada_kernel,
        grid=(L, N // tn),
        in_specs=[pl.BlockSpec((R, D), lambda l, j: (0, 0)),
                  pl.BlockSpec((None, D, tn), lambda l, j: (l, 0, j)),
                  pl.BlockSpec((None, 1, tn), lambda l, j: (l, 0, j))],
        out_specs=pl.BlockSpec((None, R, tn), lambda l, j: (l, 0, j)),
        out_shape=jax.ShapeDtypeStruct((L, R, N), F32),
        compiler_params=_cparams(),
        name="ada_mod",
    )(c_all, w_ada, b_ada.reshape(L, 1, N))


class _Rows:
    def __init__(self, B, T, SB, ST):
        self.B, self.T, self.SB, self.ST = B, T, SB, ST
        self.n_prompt = B * T
        self.n_sample = SB * SPAD
        self.M = self.n_prompt + self.n_sample
        assert T % ROW_BLK == 0 and self.n_sample % ROW_BLK == 0 and ST <= SPAD
        self.npb = self.n_prompt // ROW_BLK
        self.nsb = self.n_sample // ROW_BLK
        self.nblk = self.npb + self.nsb
        self.blk_per_seq = T // ROW_BLK


def _mod_specs(rows, D, col):
    npb, bps, B = rows.npb, rows.blk_per_seq, rows.B
    p_spec = pl.BlockSpec((None, 1, D), lambda i, *_: (jnp.minimum(i // bps, B - 1), 0, col))
    s_spec = pl.BlockSpec((ROW_BLK // SPAD, D), lambda i, *_: (jnp.maximum(i - npb, 0), col))
    return p_spec, s_spec


def _pick_mod(i, npb, p_ref, s_ref):
    s = s_ref[...]
    s_rows = jnp.concatenate(
        [jnp.broadcast_to(s[r:r + 1, :], (SPAD, s.shape[1])) for r in range(s.shape[0])], axis=0)
    return jnp.where(i < npb, p_ref[...], s_rows)


def _rms(x, g):
    return x * lax.rsqrt(jnp.mean(x * x, axis=-1, keepdims=True) + EPS) * g


def _norm_mod_kernel(x_ref, g_ref, shp_ref, shs_ref, scp_ref, scs_ref, h_ref, *, npb):
    i = pl.program_id(0)
    y = _rms(x_ref[...], g_ref[...])
    sh = _pick_mod(i, npb, shp_ref, shs_ref)
    sc = _pick_mod(i, npb, scp_ref, scs_ref)
    h_ref[...] = (y * (1.0 + sc) + sh).astype(BF16)


def _norm_mod(rows, x, g, l, modp, mods):
    M, D = x.shape
    shp, shs = _mod_specs(rows, D, 0)
    scp, scs = _mod_specs(rows, D, 1)
    row_spec = pl.BlockSpec((ROW_BLK, D), lambda i: (i, 0))
    return pl.pallas_call(
        functools.partial(_norm_mod_kernel, npb=rows.npb),
        grid=(rows.nblk,),
        in_specs=[row_spec, pl.BlockSpec((None, 1, D), lambda i: (l, 0, 0)), shp, shs, scp, scs],
        out_specs=row_spec,
        out_shape=jax.ShapeDtypeStruct((M, D), BF16),
        compiler_params=_cparams(),
        name="norm_mod",
    )(x, g, modp, mods, modp, mods)


def _mm_kernel(*refs, n_lhs, with_tail, w_is_nk):
    x_refs, w_ref, o_ref = refs[:n_lhs], refs[n_lhs], refs[n_lhs + 1]
    wbf_ref = refs[-1]

    @pl.when(pl.program_id(1) == 0)
    def _():
        if w_is_nk:
            K = wbf_ref.shape[0]
            for k0 in range(0, K, MM_XPOSE):
                wbf_ref[k0:k0 + MM_XPOSE, :] = w_ref[:, k0:k0 + MM_XPOSE].T.astype(BF16)
        else:
            wbf_ref[...] = w_ref[...].astype(BF16)

    acc = None
    k0 = 0
    for x_ref in x_refs:
        kk = x_ref.shape[1]
        part = jnp.dot(x_ref[...], wbf_ref[k0:k0 + kk, :], preferred_element_type=F32)
        acc = part if acc is None else acc + part
        k0 += kk
    o_ref[...] = acc.astype(o_ref.dtype)
    if with_tail:
        @pl.when(pl.program_id(1) == pl.num_programs(1) - 1)
        def _():
            refs[n_lhs + 2][...] = acc


def _matmul(xs, w, w_lead, col_blk0, n_cols, tn, tm, out_dtype, name, with_tail=False, w_is_nk=False):
    M = xs[0].shape[0]
    K = sum(x.shape[1] for x in xs)
    if w_is_nk:
        assert K % MM_XPOSE == 0
        if w.ndim == 3:
            w_spec = pl.BlockSpec((None, tn, K), lambda j, i: (w_lead, col_blk0 + j, 0))
        else:
            w_spec = pl.BlockSpec((tn, K), lambda j, i: (col_blk0 + j, 0))
    elif w.ndim == 3:
        w_spec = pl.BlockSpec((None, K, tn), lambda j, i: (w_lead, 0, col_blk0 + j))
    else:
        w_spec = pl.BlockSpec((K, tn), lambda j, i: (0, col_blk0 + j))
    x_specs = [pl.BlockSpec((tm, x.shape[1]), lambda j, i: (i, 0)) for x in xs]
    out_specs = [pl.BlockSpec((tm, tn), lambda j, i: (i, j))]
    out_shape = [jax.ShapeDtypeStruct((M, n_cols), out_dtype)]
    if with_tail:
        out_specs.append(pl.BlockSpec((tm, tn), lambda j, i: (0, j)))
        out_shape.append(jax.ShapeDtypeStruct((tm, n_cols), F32))
    return pl.pallas_call(
        functools.partial(_mm_kernel, n_lhs=len(xs), with_tail=with_tail, w_is_nk=w_is_nk),
        grid=(n_cols // tn, M // tm),
        in_specs=x_specs + [w_spec],
        out_specs=out_specs,
        out_shape=out_shape,
        scratch_shapes=[pltpu.VMEM((K, tn), BF16)],
        compiler_params=_cparams(),
        name=name,
    )(*xs, w)


def _gla_kernel(*refs, C, valid, has_s0, dk, dv):
    q_ref, k_ref, v_ref, gg_ref, glr_ref, wgu_ref, bg_ref, go_ref = refs[:8]
    s0_ref = refs[8] if has_s0 else None
    o_ref, sf_ref, S = refs[-3:]
    c = pl.program_id(2)

    @pl.when(c == 0)
    def _():
        if has_s0:
            S[...] = s0_ref[...]
        else:
            S[...] = jnp.zeros_like(S)

    z = jnp.dot(glr_ref[...], wgu_ref[...], preferred_element_type=F32) + bg_ref[...]
    la = (jnp.minimum(z, 0.0) - jnp.log(1.0 + jnp.exp(-jnp.abs(z)))) * (1.0 / GATE_TEMP)
    q = q_ref[...].astype(F32) * (dk ** -0.5)
    k = k_ref[...].astype(F32)
    if valid < C:
        live = lax.broadcasted_iota(jnp.int32, (C, 1), 0) < valid
        la = jnp.where(live, la, 0.0)
        k = jnp.where(live, k, 0.0)

    la_hi = la.astype(BF16).astype(F32)
    la_lo = (la - la_hi).astype(BF16).astype(F32)
    ri = lax.broadcasted_iota(jnp.int32, (C, C), 0)
    ci = lax.broadcasted_iota(jnp.int32, (C, C), 1)
    tril = ri >= ci
    trilf = tril.astype(F32)
    b = (jnp.dot(trilf, la_hi, preferred_element_type=F32)
         + jnp.dot(trilf, la_lo, preferred_element_type=F32))

    q_dec = (q * jnp.exp(b)).astype(BF16)
    if C > SPAD:
        r = b[C // 2:C // 2 + 1, :]
        q_in = (q * jnp.exp(b - r)).astype(BF16)
        k_in = (k * jnp.exp(r - b)).astype(BF16)
    else:
        q_in = q_dec
        k_in = (k * jnp.exp(-b)).astype(BF16)
    att = lax.dot_general(q_in, k_in, (((1,), (1,)), ((), ())), preferred_element_type=F32)
    att = jnp.where(tril, att, 0.0)

    s_old = S[...]
    v = v_ref[...].astype(BF16)
    o = (jnp.dot(att.astype(BF16), v, preferred_element_type=F32)
         + jnp.dot(q_dec, s_old.astype(BF16), preferred_element_type=F32))

    b_end = b[valid - 1:valid, :]
    k_end = (k * jnp.exp(b_end - b)).astype(BF16)
    tn_dims = (((0,), (0,)), ((), ()))
    upd = lax.dot_general(k_end, v, tn_dims, preferred_element_type=F32)
    ones = jnp.ones((C, LANES), F32)
    b_col = (lax.dot_general(la_hi, ones, tn_dims, preferred_element_type=F32)
             + lax.dot_general(la_lo, ones, tn_dims, preferred_element_type=F32))
    dec = jnp.exp(b_col)
    S[...] = jnp.concatenate([dec] * (dv // LANES), axis=1) * s_old + upd

    gg = gg_ref[...].astype(F32)
    on = _rms(o, go_ref[...])
    o_ref[...] = (on * (gg * (1.0 / (1.0 + jnp.exp(-gg))))).astype(BF16)

    @pl.when(c == pl.num_programs(2) - 1)
    def _():
        sf_ref[...] = S[...]


def _gla(proj_a, glr, wgu, bg, go, l, *, M, L, n_seq, n_chunks, C, valid, row_blk0, in_row_blk0=None,
         s0=None, o_prev=None, stack=None):
    H = GLA_HEADS
    dk = wgu.shape[2] // H
    dv = go.shape[2]
    has_s0 = s0 is not None
    if in_row_blk0 is None:
        in_row_blk0 = row_blk0

    def rb(b, c):
        return row_blk0 + b * n_chunks + c

    def rbi(b, c):
        return in_row_blk0 + b * n_chunks + c

    in_specs = [
        pl.BlockSpec((C, dk), lambda b, h, c: (rbi(b, c), h)),
        pl.BlockSpec((C, dk), lambda b, h, c: (rbi(b, c), H + h)),
        pl.BlockSpec((C, dv), lambda b, h, c: (rbi(b, c), (2 * H * dk) // dv + h)),
        pl.BlockSpec((C, dv), lambda b, h, c: (rbi(b, c), (2 * H * dk) // dv + H + h)),
        pl.BlockSpec((C, LANES), lambda b, h, c: (rb(b, c), 0)),
        pl.BlockSpec((None, LANES, dk), lambda b, h, c: (l, 0, h)),
        pl.BlockSpec((None, 1, dk), lambda b, h, c: (l, 0, h)),
        pl.BlockSpec((None, 1, dv), lambda b, h, c: (l, 0, 0)),
    ]
    args = [proj_a, proj_a, proj_a, proj_a, glr, wgu, bg, go]
    aliases = {}
    if has_s0:
        in_specs += [pl.BlockSpec((None, None, None, dk, dv), lambda b, h, c: (l, b, h, 0, 0)),
                     pl.BlockSpec(memory_space=pl.ANY)]
        args += [s0, o_prev]
        aliases = {len(args) - 1: 0}
    if stack is not None:
        in_specs.append(pl.BlockSpec(memory_space=pl.ANY))
        args.append(stack)
        aliases[len(args) - 1] = 1
    return pl.pallas_call(
        functools.partial(_gla_kernel, C=C, valid=valid, has_s0=has_s0, dk=dk, dv=dv),
        grid=(n_seq, H, n_chunks),
        in_specs=in_specs,
        out_specs=[pl.BlockSpec((C, dv), lambda b, h, c: (rb(b, c), h)),
                   pl.BlockSpec((None, None, None, dk, dv), lambda b, h, c: (l, b, h, 0, 0))],
        out_shape=[jax.ShapeDtypeStruct((M, H * dv), BF16),
                   jax.ShapeDtypeStruct((L, n_seq, H, dk, dv), F32)],
        scratch_shapes=[pltpu.VMEM((dk, dv), F32)],
        input_output_aliases=aliases,
        compiler_params=_cparams(),
        name="gla_sample" if has_s0 else "gla_prompt",
    )(*args)


def _head_norm(x, g):
    return x * lax.rsqrt(jnp.mean(x * x, axis=-1, keepdims=True) + EPS) * g


def _segment_rms_inv(x, seg):
    R, Wd = x.shape
    assert Wd // seg <= LANES
    sel = (lax.broadcasted_iota(jnp.int32, (Wd, LANES), 0) // seg
           == lax.broadcasted_iota(jnp.int32, (Wd, LANES), 1)).astype(F32).astype(BF16)
    sel_t = (lax.broadcasted_iota(jnp.int32, (LANES, Wd), 0)
             == lax.broadcasted_iota(jnp.int32, (LANES, Wd), 1) // seg).astype(F32).astype(BF16)

    def split(a):
        hi = a.astype(BF16)
        return hi, (a - hi.astype(F32)).astype(BF16)

    sq_hi, sq_lo = split(x * x)
    sums = (jnp.dot(sq_hi, sel, preferred_element_type=F32)
            + jnp.dot(sq_lo, sel, preferred_element_type=F32))
    inv_hi, inv_lo = split(lax.rsqrt(sums * (1.0 / seg) + EPS))
    return (jnp.dot(inv_hi, sel_t, preferred_element_type=F32)
            + jnp.dot(inv_lo, sel_t, preferred_element_type=F32))


def _softmax_sink_pv(parts, sink):
    m = sink
    for s, _ in parts:
        m = jnp.maximum(m, jnp.max(s, axis=-1, keepdims=True))
    den = jnp.exp(sink - m)
    es = []
    for s, _ in parts:
        e = jnp.exp(s - m)
        den = den + jnp.sum(e, axis=-1, keepdims=True)
        es.append(e)
    inv = 1.0 / den
    acc = None
    for e, (_, v) in zip(es, parts):
        pv = jnp.dot((e * inv).astype(BF16), v, preferred_element_type=F32)
        acc = pv if acc is None else acc + pv
    return acc


NT_DIMS = (((1,), (1,)), ((), ()))


def _swa_prompt_kernel(sink_ref, q_ref, kp_ref, kc_ref, vp_ref, vc_ref, gq_ref, gk_ref,
                       o_ref, ko_ref, vo_ref, *, kvh):
    i = pl.program_id(1)
    hd = SWA_HEAD_DIM
    W = WINDOW
    G = SWA_GROUP
    r = lax.broadcasted_iota(jnp.int32, (G * W, 2 * W), 0) % W
    j = lax.broadcasted_iota(jnp.int32, (G * W, 2 * W), 1)
    mask = (j >= r) & (j <= r + W) & ((j >= W) | (i > 0))
    head_of_row = lax.broadcasted_iota(jnp.int32, (G * W, 1), 0) // W
    gk = gk_ref[...]
    q = q_ref[...].astype(F32)
    gq_row = jnp.concatenate([gq_ref[...]] * (q.shape[1] // hd), axis=1)
    qn = (q * _segment_rms_inv(q, hd) * gq_row * (hd ** -0.5)).astype(BF16)
    kp = kp_ref[...].astype(F32)
    kc = kc_ref[...].astype(F32)
    k_new = []
    outs = []
    for kv in range(kvh):
        sl = slice(kv * hd, (kv + 1) * hd)
        kcn = _head_norm(kc[:, sl], gk)
        k_new.append(kcn)
        kcat = jnp.concatenate([_head_norm(kp[:, sl], gk), kcn], axis=0).astype(BF16)
        vcat = jnp.concatenate([vp_ref[:, sl], vc_ref[:, sl]], axis=0)
        h0 = kv * G
        qg = jnp.concatenate([qn[:, (h0 + g) * hd:(h0 + g + 1) * hd] for g in range(G)], axis=0)
        sink = jnp.zeros((G * W, 1), F32)
        for g in range(G):
            sink = jnp.where(head_of_row == g, sink_ref[h0 + g], sink)
        s = lax.dot_general(qg, kcat, NT_DIMS, preferred_element_type=F32)
        s = jnp.where(mask, s, NEG_INF)
        og = _softmax_sink_pv([(s, vcat)], sink)
        outs += [og[g * W:(g + 1) * W, :] for g in range(G)]
    o_ref[...] = jnp.concatenate(outs, axis=1).astype(BF16)
    ko_ref[...] = jnp.concatenate(k_new, axis=1)
    vo_ref[...] = vc_ref[...].astype(F32)


def _swa_prompt(proj_b, sinks_l, gq, gk, l, *, B, T, kvh):
    M = proj_b.shape[0]
    hd = SWA_HEAD_DIM
    qw = kvh * SWA_GROUP * hd
    kw = kvh * hd
    nb = T // WINDOW
    kblk = qw // kw

    def cur(b, i, s):
        return b * nb + i

    def prev(b, i, s):
        return b * nb + jnp.maximum(i - 1, 0)

    grid_spec = pltpu.PrefetchScalarGridSpec(
        num_scalar_prefetch=1,
        grid=(B, nb),
        in_specs=[
            pl.BlockSpec((WINDOW, qw), lambda b, i, s: (cur(b, i, s), 0)),
            pl.BlockSpec((WINDOW, kw), lambda b, i, s: (prev(b, i, s), kblk)),
            pl.BlockSpec((WINDOW, kw), lambda b, i, s: (cur(b, i, s), kblk)),
            pl.BlockSpec((WINDOW, kw), lambda b, i, s: (prev(b, i, s), kblk + 1)),
            pl.BlockSpec((WINDOW, kw), lambda b, i, s: (cur(b, i, s), kblk + 1)),
            pl.BlockSpec((None, 1, hd), lambda b, i, s: (l, 0, 0)),
            pl.BlockSpec((None, 1, hd), lambda b, i, s: (l, 0, 0)),
        ],
        out_specs=[
            pl.BlockSpec((WINDOW, qw), lambda b, i, s: (cur(b, i, s), 0)),
            pl.BlockSpec((None, WINDOW, kw), lambda b, i, s: (b, 0, 0)),
            pl.BlockSpec((None, WINDOW, kw), lambda b, i, s: (b, 0, 0)),
        ],
    )
    return pl.pallas_call(
        functools.partial(_swa_prompt_kernel, kvh=kvh),
        grid_spec=grid_spec,
        out_shape=[jax.ShapeDtypeStruct((M, qw), BF16),
                   jax.ShapeDtypeStruct((B, WINDOW, kw), F32),
                   jax.ShapeDtypeStruct((B, WINDOW, kw), F32)],
        compiler_params=_cparams(),
        name="swa_prompt",
    )(sinks_l, proj_b, proj_b, proj_b, proj_b, proj_b, gq, gk)


def _swa_sample_kernel(sink_ref, q_ref, kn_ref, vn_ref, kc_ref, vc_ref, gq_ref, gk_ref, _o_prev,
                       o_ref, ko_ref, vo_ref, *, kvh, st, nbuf):
    hd = SWA_HEAD_DIM
    G = SWA_GROUP
    r1 = lax.broadcasted_iota(jnp.int32, (G * SPAD, nbuf), 0) % SPAD
    j1 = lax.broadcasted_iota(jnp.int32, (G * SPAD, nbuf), 1)
    mask_buf = (nbuf + r1 - j1 >= 0) & (nbuf + r1 - j1 <= WINDOW)
    r2 = lax.broadcasted_iota(jnp.int32, (G * SPAD, SPAD), 0) % SPAD
    t2 = lax.broadcasted_iota(jnp.int32, (G * SPAD, SPAD), 1)
    mask_new = (t2 <= r2) & (t2 < st)
    head_of_row = lax.broadcasted_iota(jnp.int32, (G * SPAD, 1), 0) // SPAD
    gq = gq_ref[...]
    gk = gk_ref[...]
    kn = kn_ref[...].astype(F32)
    k_new = []
    outs = []
    for kv in range(kvh):
        sl = slice(kv * hd, (kv + 1) * hd)
        knn = _head_norm(kn[:, sl], gk)
        k_new.append(knn)
        kbuf = kc_ref[:, sl].astype(BF16)
        vbuf = vc_ref[:, sl].astype(BF16)
        knb = knn.astype(BF16)
        vnb = vn_ref[:, sl].astype(BF16)
        h0 = kv * G
        qg = jnp.concatenate(
            [_head_norm(q_ref[:, (h0 + g) * hd:(h0 + g + 1) * hd].astype(F32), gq) for g in range(G)], axis=0)
        qg = (qg * (hd ** -0.5)).astype(BF16)
        sink = jnp.zeros((G * SPAD, 1), F32)
        for g in range(G):
            sink = jnp.where(head_of_row == g, sink_ref[h0 + g], sink)
        s1 = lax.dot_general(qg, kbuf, NT_DIMS, preferred_element_type=F32)
        s2 = lax.dot_general(qg, knb, NT_DIMS, preferred_element_type=F32)
        s1 = jnp.where(mask_buf, s1, NEG_INF)
        s2 = jnp.where(mask_new, s2, NEG_INF)
        og = _softmax_sink_pv([(s1, vbuf), (s2, vnb)], sink)
        outs += [og[g * SPAD:(g + 1) * SPAD, :] for g in range(G)]
    o_ref[...] = jnp.concatenate(outs, axis=1).astype(BF16)
    k_all = jnp.concatenate(k_new, axis=1)
    ko_ref[...] = jnp.concatenate([kc_ref[st:, :], k_all[:st, :]], axis=0)
    vo_ref[...] = jnp.concatenate([vc_ref[st:, :], vn_ref[:st, :].astype(F32)], axis=0)


def _swa_sample(proj_b, o_prev, cache_k, cache_v, sinks_l, gq, gk, l, *, SB, ST, kvh, row_blk0, in_row_blk0):
    M = o_prev.shape[0]
    hd = SWA_HEAD_DIM
    qw = kvh * SWA_GROUP * hd
    kw = kvh * hd
    nbuf = cache_k.shape[2]
    kblk = qw // kw
    grid_spec = pltpu.PrefetchScalarGridSpec(
        num_scalar_prefetch=1,
        grid=(SB,),
        in_specs=[
            pl.BlockSpec((SPAD, qw), lambda b, s: (in_row_blk0 + b, 0)),
            pl.BlockSpec((SPAD, kw), lambda b, s: (in_row_blk0 + b, kblk)),
            pl.BlockSpec((SPAD, kw), lambda b, s: (in_row_blk0 + b, kblk + 1)),
            pl.BlockSpec((None, None, nbuf, kw), lambda b, s: (l, b, 0, 0)),
            pl.BlockSpec((None, None, nbuf, kw), lambda b, s: (l, b, 0, 0)),
            pl.BlockSpec((None, 1, hd), lambda b, s: (l, 0, 0)),
            pl.BlockSpec((None, 1, hd), lambda b, s: (l, 0, 0)),
            pl.BlockSpec(memory_space=pl.ANY),
        ],
        out_specs=[
            pl.BlockSpec((SPAD, qw), lambda b, s: (row_blk0 + b, 0)),
            pl.BlockSpec((None, nbuf, kw), lambda b, s: (b, 0, 0)),
            pl.BlockSpec((None, nbuf, kw), lambda b, s: (b, 0, 0)),
        ],
    )
    return pl.pallas_call(
        functools.partial(_swa_sample_kernel, kvh=kvh, st=ST, nbuf=nbuf),
        grid_spec=grid_spec,
        out_shape=[jax.ShapeDtypeStruct((M, qw), BF16),
                   jax.ShapeDtypeStruct((SB, nbuf, kw), F32),
                   jax.ShapeDtypeStruct((SB, nbuf, kw), F32)],
        input_output_aliases={8: 0},
        compiler_params=_cparams(),
        name="swa_sample",
    )(sinks_l, proj_b, proj_b, proj_b, cache_k, cache_v, gq, gk, o_prev)


def _post_attn_kernel(x_ref, mix_ref, g_ref, gtp_ref, gts_ref, shp_ref, shs_ref, scp_ref, scs_ref,
                      wr_ref, x1_ref, h2_ref, lg_ref, *, npb):
    i = pl.program_id(0)
    gt = _pick_mod(i, npb, gtp_ref, gts_ref)
    x1 = x_ref[...] + gt * mix_ref[...]
    x1_ref[...] = x1
    sh = _pick_mod(i, npb, shp_ref, shs_ref)
    sc = _pick_mod(i, npb, scp_ref, scs_ref)
    h2 = _rms(x1, g_ref[...]) * (1.0 + sc) + sh
    hb = h2.astype(BF16)
    h2_ref[...] = hb
    lg_ref[...] = jnp.dot(hb, wr_ref[...], preferred_element_type=F32)


def _post_attn(rows, x, mix, g, l, modp, mods, wr):
    M, D = x.shape
    gtp, gts = _mod_specs(rows, D, 2)
    shp, shs = _mod_specs(rows, D, 3)
    scp, scs = _mod_specs(rows, D, 4)
    row_spec = pl.BlockSpec((ROW_BLK, D), lambda i: (i, 0))
    wr_spec = pl.BlockSpec((D, LANES), lambda i: (0, 0))
    return pl.pallas_call(
        functools.partial(_post_attn_kernel, npb=rows.npb),
        grid=(rows.nblk,),
        in_specs=[row_spec, row_spec, pl.BlockSpec((None, 1, D), lambda i: (l, 0, 0)),
                  gtp, gts, shp, shs, scp, scs, wr_spec],
        out_specs=[row_spec, row_spec, pl.BlockSpec((ROW_BLK, LANES), lambda i: (i, 0))],
        out_shape=[jax.ShapeDtypeStruct((M, D), F32),
                   jax.ShapeDtypeStruct((M, D), BF16),
                   jax.ShapeDtypeStruct((M, LANES), F32)],
        compiler_params=_cparams(),
        name="post_attn",
    )(x, mix, g, modp, mods, modp, mods, modp, mods, wr)


def _first_block_of_expert(be_ref, i):
    return (i == 0) | (be_ref[i] != be_ref[jnp.maximum(i - 1, 0)])


def _moe_up_kernel(be_ref, na_ref, x_ref, wg_ref, wu_ref, h_ref, wgb, wub):
    i = pl.program_id(1)

    @pl.when(_first_block_of_expert(be_ref, i))
    def _():
        wgb[...] = wg_ref[...].astype(BF16)
        wub[...] = wu_ref[...].astype(BF16)

    @pl.when(i < na_ref[0])
    def _():
        x = x_ref[...]
        a = jnp.dot(x, wgb[...], preferred_element_type=F32)
        u = jnp.dot(x, wub[...], preferred_element_type=F32)
        h_ref[...] = (a * (1.0 / (1.0 + jnp.exp(-a))) * u).astype(BF16)

    @pl.when(i >= na_ref[0])
    def _():
        h_ref[...] = jnp.zeros_like(h_ref)


def _moe_down_kernel(be_ref, na_ref, h_ref, wd_ref, y_ref, wdb):
    i = pl.program_id(1)

    @pl.when(_first_block_of_expert(be_ref, i))
    def _():
        wdb[...] = wd_ref[...].astype(BF16)

    @pl.when(i < na_ref[0])
    def _():
        y_ref[...] = jnp.dot(h_ref[...], wdb[...], preferred_element_type=F32)

    @pl.when(i >= na_ref[0])
    def _():
        y_ref[...] = jnp.zeros_like(y_ref)


def _moe_ffn(xs, block_e, n_active, w_g, w_u, w_d, l):
    P, D = xs.shape
    F = w_g.shape[3]
    nb = P // MOE_BLK
    up_spec = pltpu.PrefetchScalarGridSpec(
        num_scalar_prefetch=2,
        grid=(F // MOE_FC, nb),
        in_specs=[pl.BlockSpec((MOE_BLK, D), lambda j, i, be, na: (i, 0)),
                  pl.BlockSpec((None, None, D, MOE_FC), lambda j, i, be, na: (l, be[i], 0, j)),
                  pl.BlockSpec((None, None, D, MOE_FC), lambda j, i, be, na: (l, be[i], 0, j))],
        out_specs=pl.BlockSpec((MOE_BLK, MOE_FC), lambda j, i, be, na: (i, j)),
        scratch_shapes=[pltpu.VMEM((D, MOE_FC), BF16), pltpu.VMEM((D, MOE_FC), BF16)],
    )
    hmid = pl.pallas_call(
        _moe_up_kernel, grid_spec=up_spec,
        out_shape=jax.ShapeDtypeStruct((P, F), BF16),
        compiler_params=_cparams(), name="moe_up",
    )(block_e, n_active, xs, w_g, w_u)
    down_spec = pltpu.PrefetchScalarGridSpec(
        num_scalar_prefetch=2,
        grid=(D // MOE_DC, nb),
        in_specs=[pl.BlockSpec((MOE_BLK, F), lambda j, i, be, na: (i, 0)),
                  pl.BlockSpec((None, None, F, MOE_DC), lambda j, i, be, na: (l, be[i], 0, j))],
        out_specs=pl.BlockSpec((MOE_BLK, MOE_DC), lambda j, i, be, na: (i, j)),
        scratch_shapes=[pltpu.VMEM((F, MOE_DC), BF16)],
    )
    return pl.pallas_call(
        _moe_down_kernel, grid_spec=down_spec,
        out_shape=jax.ShapeDtypeStruct((P, D), F32),
        compiler_params=_cparams(), name="moe_down",
    )(block_e, n_active, hmid, w_d)


def _route(logits, b_router, valid_row):
    E = b_router.shape[0]
    assert TOP_K == 2
    epg = E // N_GROUPS
    probs = jax.nn.softmax(logits + b_router.astype(F32), axis=-1)
    pg = probs.reshape(-1, N_GROUPS, epg)
    lane = jnp.arange(epg, dtype=jnp.int32)
    m1 = jnp.max(pg, axis=-1, keepdims=True)
    i1 = jnp.min(jnp.where(pg == m1, lane, epg), axis=-1, keepdims=True)
    rest = jnp.where(lane == i1, -jnp.inf, pg)
    m2 = jnp.max(rest, axis=-1, keepdims=True)
    i2 = jnp.min(jnp.where(rest == m2, lane, epg), axis=-1, keepdims=True)
    group_score = (m1 + m2)[..., 0]
    g_sel = jnp.argmax(group_score, axis=-1)
    pick = (jnp.arange(N_GROUPS, dtype=jnp.int32)[None, :] == g_sel[:, None])[..., None]
    vals = jnp.sum(jnp.where(pick, jnp.concatenate([m1, m2], axis=-1), 0.0), axis=1)
    idx = jnp.sum(jnp.where(pick, jnp.concatenate([i1, i2], axis=-1), 0), axis=1)
    expert_idx = (g_sel[:, None] * epg + idx).astype(jnp.int32)
    gates = vals / jnp.sum(vals, axis=-1, keepdims=True)
    expert_idx = jnp.where(valid_row[:, None], expert_idx, E)
    gates = jnp.where(valid_row[:, None], gates, 0.0)
    return expert_idx, gates


def _dispatch(expert_idx, n_valid, E):
    M = expert_idx.shape[0]
    MK = M * TOP_K
    flat_e = expert_idx.reshape(MK)
    flat_tok = jnp.repeat(jnp.arange(M, dtype=jnp.int32), TOP_K)
    onehot = (flat_e[:, None] == jnp.arange(E, dtype=jnp.int32)[None, :]).astype(jnp.int32)
    oh3 = onehot.reshape(MK // LANES, LANES, E)
    within = jnp.cumsum(oh3, axis=1)
    blk_tot = within[:, -1, :]
    blk_off = jnp.cumsum(blk_tot, axis=0) - blk_tot
    rank = jnp.sum((within + blk_off[:, None, :] - 1) * oh3, axis=-1).reshape(MK)
    counts = jnp.sum(blk_tot, axis=0)
    padded = (counts + MOE_BLK - 1) // MOE_BLK * MOE_BLK
    pad_end = jnp.cumsum(padded)
    pad_start = pad_end - padded
    nb = (n_valid * TOP_K + E * (MOE_BLK - 1)) // MOE_BLK + 1
    P = nb * MOE_BLK
    seg_start = jnp.sum(onehot * pad_start[None, :], axis=-1)
    dest = jnp.where(flat_e < E, seg_start + rank, P)
    slot_tok = jnp.zeros((P,), jnp.int32).at[dest].set(flat_tok, mode="drop")
    blk_row0 = jnp.arange(nb, dtype=jnp.int32) * MOE_BLK
    block_e = jnp.minimum(jnp.sum(blk_row0[:, None] >= pad_end[None, :], axis=-1), E - 1).astype(jnp.int32)
    n_active = (pad_end[-1:] // MOE_BLK).astype(jnp.int32)
    pair_dest = jnp.minimum(dest, P - 1).reshape(M, TOP_K)
    return slot_tok, block_e, n_active, pair_dest


def _post_moe_kernel(pd_ref, x_ref, ys_hbm, gw_ref, gtp_ref, gts_ref, *rest, npb, with_norm):
    ybuf, sem = rest[-2:]
    rest = rest[:-2]
    i = pl.program_id(0)
    nsteps = pl.num_programs(0)

    def row_copy(step, slot, r, k):
        d = pd_ref[(step * ROW_BLK + r) * TOP_K + k]
        return pltpu.make_async_copy(ys_hbm.at[d], ybuf.at[slot, k, r], sem.at[slot])

    def issue(step, slot):
        def body(r, carry):
            for k in range(TOP_K):
                row_copy(step, slot, r, k).start()
            return carry
        lax.fori_loop(0, ROW_BLK, body, 0, unroll=8)

    @pl.when(i == 0)
    def _():
        issue(0, 0)

    @pl.when(i + 1 < nsteps)
    def _():
        issue(i + 1, (i + 1) % 2)

    slot = i % 2
    pltpu.make_async_copy(ybuf.at[slot], ybuf.at[slot], sem.at[slot]).wait()

    gt = _pick_mod(i, npb, gtp_ref, gts_ref)
    gw = gw_ref[...]
    y = gw[:, 0:1] * ybuf[slot, 0] + gw[:, 1:2] * ybuf[slot, 1]
    x2 = x_ref[...] + gt * y
    if with_norm:
        g_ref, shp_ref, shs_ref, scp_ref, scs_ref, x2_ref, h_ref = rest
        x2_ref[...] = x2
        sh = _pick_mod(i, npb, shp_ref, shs_ref)
        sc = _pick_mod(i, npb, scp_ref, scs_ref)
        h_ref[...] = (_rms(x2, g_ref[...]) * (1.0 + sc) + sh).astype(BF16)
    else:
        (x2_ref,) = rest
        x2_ref[...] = x2


def _post_moe(rows, x1, ys, pair_dest, gates_pad, l, modp, mods, g_next=None, modp_next=None, mods_next=None):
    M, D = x1.shape
    with_norm = g_next is not None
    gtp, gts = _mod_specs(rows, D, 5)
    row_spec = pl.BlockSpec((ROW_BLK, D), lambda i, *_: (i, 0))
    in_specs = [row_spec, pl.BlockSpec(memory_space=pl.ANY),
                pl.BlockSpec((ROW_BLK, LANES), lambda i, *_: (i, 0)), gtp, gts]
    args = [x1, ys, gates_pad, modp, mods]
    out_specs = [row_spec]
    out_shape = [jax.ShapeDtypeStruct((M, D), F32)]
    if with_norm:
        shp, shs = _mod_specs(rows, D, 0)
        scp, scs = _mod_specs(rows, D, 1)
        in_specs += [pl.BlockSpec((None, 1, D), lambda i, *_: (l + 1, 0, 0)), shp, shs, scp, scs]
        args += [g_next, modp_next, mods_next, modp_next, mods_next]
        out_specs.append(row_spec)
        out_shape.append(jax.ShapeDtypeStruct((M, D), BF16))
    grid_spec = pltpu.PrefetchScalarGridSpec(
        num_scalar_prefetch=1,
        grid=(rows.nblk,),
        in_specs=in_specs, out_specs=out_specs,
        scratch_shapes=[pltpu.VMEM((2, TOP_K, ROW_BLK, D), F32), pltpu.SemaphoreType.DMA((2,))],
    )
    return pl.pallas_call(
        functools.partial(_post_moe_kernel, npb=rows.npb, with_norm=with_norm),
        grid_spec=grid_spec, out_shape=out_shape,
        compiler_params=_cparams(),
        name="post_moe",
    )(pair_dest.reshape(-1), *args)


def kernel(x_prompt, x_sample, c_prompt, c_sample, state_gla, cache_swa_k, cache_swa_v, w_ada, b_ada, g_norm1, g_norm2, w_in, w_gate_up, b_gate, g_gla_out, g_q_norm, g_k_norm, sinks, w_out, w_router, b_router, w_e_gate, w_e_up, w_e_down):
    B, T, D = x_prompt.shape
    SB, ST, _ = x_sample.shape
    L = w_ada.shape[0]
    H = GLA_HEADS
    dk = state_gla.shape[3]
    dv = state_gla.shape[4]
    nbuf, kvh, hd = cache_swa_k.shape[2:]
    E = w_router.shape[1]
    rows = _Rows(B, T, SB, ST)
    M = rows.M

    gla_cols = 2 * H * dk + 2 * H * dv
    swa_col0 = gla_cols + GATE_RANK
    swa_cols = w_in.shape[2] - swa_col0

    xs_pad = jnp.pad(x_sample, ((0, 0), (0, SPAD - ST), (0, 0))).reshape(SB * SPAD, D)
    x = jnp.concatenate([x_prompt.reshape(B * T, D), xs_pad], axis=0)
    row_id = jnp.arange(M, dtype=jnp.int32)
    valid_row = (row_id < rows.n_prompt) | ((row_id - rows.n_prompt) % SPAD < ST)
    n_valid = rows.n_prompt + SB * ST

    mod = _ada_mod(jnp.concatenate([c_prompt, c_sample], axis=0), w_ada, b_ada)
    modp_all = mod[:, :B].reshape(L, B, 1, 6 * D)
    mods_all = mod[:, B:]

    wgu_pad = jnp.pad(w_gate_up, ((0, 0), (0, LANES - GATE_RANK), (0, 0))).astype(BF16)
    bg3 = b_gate.reshape(L, 1, H * dk)
    go3 = g_gla_out.reshape(L, 1, dv)
    gq3 = g_q_norm.reshape(L, 1, hd)
    gk3 = g_k_norm.reshape(L, 1, hd)
    g1 = g_norm1.reshape(L, 1, D)
    g2 = g_norm2.reshape(L, 1, D)
    wr = jnp.pad(w_router, ((0, 0), (0, LANES - E))).astype(BF16)
    w_in_t = jnp.swapaxes(w_in, 1, 2)
    tm = MM_ROWS
    tail_blk0 = (rows.n_prompt - (M - tm)) // SPAD
    assert M % tm == 0 and M - tm <= rows.n_prompt and (rows.n_prompt - (M - tm)) % SPAD == 0
    ck =cache_swa_k.reshape(L, SB, nbuf, kvh * hd)
    cv = cache_swa_v.reshape(L, SB, nbuf, kvh * hd)

    h = _norm_mod(rows, x, g1, 0, modp_all[0], mods_all[0])
    kp_l, vp_l, ks_l, vs_l = [], [], [], []
    s_p = s_s = None
    for l in range(L):
        modp, mods = modp_all[l], mods_all[l]
        proj_a, proj_a_s = _matmul([h], w_in_t, l, 0, gla_cols, 512, tm, BF16, "proj_gla", with_tail=True,
                                   w_is_nk=True)
        (glr,) = _matmul([h], w_in_t, l, gla_cols // LANES, LANES, LANES, tm, BF16, "proj_gate_lr", w_is_nk=True)
        w_swa = w_in_t[l, swa_col0:, :]
        proj_b, proj_b_s = _matmul([h], w_swa, None, 0, swa_cols, 512, tm, BF16, "proj_swa", with_tail=True,
                                   w_is_nk=True)

        o_gla, s_p = _gla(proj_a, glr, wgu_pad, bg3, go3, l, M=M, L=L, n_seq=B, n_chunks=T // GLA_CHUNK,
                          C=GLA_CHUNK, valid=GLA_CHUNK, row_blk0=0, stack=s_p)
        o_gla, s_s = _gla(proj_a_s, glr, wgu_pad, bg3, go3, l, M=M, L=L, n_seq=SB, n_chunks=1, C=SPAD, valid=ST,
                          row_blk0=rows.n_prompt // SPAD, in_row_blk0=tail_blk0, s0=state_gla, o_prev=o_gla,
                          stack=s_s)
        o_swa, k_p, v_p = _swa_prompt(proj_b, sinks[l], gq3, gk3, l, B=B, T=T, kvh=kvh)
        o_swa, k_s, v_s = _swa_sample(proj_b_s, o_swa, ck, cv, sinks[l], gq3, gk3, l, SB=SB, ST=ST, kvh=kvh,
                                      row_blk0=rows.n_prompt // SPAD, in_row_blk0=tail_blk0)

        (mix,) = _matmul([o_gla, o_swa], w_out, l, 0, D, 512, tm, F32, "out_proj")
        x1, h2, logits = _post_attn(rows, x, mix, g2, l, modp, mods, wr)

        expert_idx, gates = _route(logits[:, :E], b_router, valid_row)
        slot_tok, block_e, n_active, pair_dest = _dispatch(expert_idx, n_valid, E)
        ys = _moe_ffn(h2[slot_tok], block_e, n_active, w_e_gate, w_e_up, w_e_down, l)
        gates_pad = jnp.pad(gates, ((0, 0), (0, LANES - TOP_K)))
        if l + 1 < L:
            x, h = _post_moe(rows, x1, ys, pair_dest, gates_pad, l, modp, mods, g1, modp_all[l + 1], mods_all[l + 1])
        else:
            (x,) = _post_moe(rows, x1, ys, pair_dest, gates_pad, l, modp, mods)

        kp_l.append(k_p.reshape(B, WINDOW, kvh, hd))
        vp_l.append(v_p.reshape(B, WINDOW, kvh, hd))
        ks_l.append(k_s.reshape(SB, nbuf, kvh, hd))
        vs_l.append(v_s.reshape(SB, nbuf, kvh, hd))

    y_prompt = x[:rows.n_prompt].reshape(B, T, D)
    y_sample = x[rows.n_prompt:].reshape(SB, SPAD, D)[:, :ST]
    return (y_prompt, y_sample, s_p, jnp.stack(kp_l), jnp.stack(vp_l),
            s_s, jnp.stack(ks_l), jnp.stack(vs_l))
```
